```python
import jax, jax.numpy as jnp
from jax import lax
import numpy as np

D_MODEL = 4096
BATCH = 2
SEQ = 8192
DEPTH = 2
DEC_BATCH = 32
DEC_SEQ = 64
PAST_LEN = 1024

CHUNK = 64
N_EVEN = (DEPTH + 1) // 2
N_ODD = DEPTH // 2
EPS = 1e-6

GDN_HEADS = 16
GDN_DK = 128
GDN_DV = 128
GDN_CONV = 4
GDN_QKV = GDN_HEADS * (2 * GDN_DK + GDN_DV)
GDN_W = GDN_HEADS * GDN_DV
GLA_HEADS = 4
GLA_DK = 256
GLA_DV = 512
GLA_RANK = 16
GLA_TAU = 16.0
GLA_W = GLA_HEADS * GLA_DV
MLSTM_HEADS = 8
MLSTM_DK = 256
MLSTM_DV = 512
MLSTM_W = MLSTM_HEADS * MLSTM_DV
D_FF = 11008
N_SUB = 3

AB_SPLITS = [GDN_QKV, GDN_W, GDN_HEADS, GDN_HEADS,
             GLA_HEADS * GLA_DK, GLA_HEADS * GLA_DK, GLA_W, GLA_W, GLA_RANK]
P_AB = sum(AB_SPLITS)
MLSTM_SPLITS = [MLSTM_HEADS * MLSTM_DK, MLSTM_HEADS * MLSTM_DK, MLSTM_W, MLSTM_W, MLSTM_HEADS, MLSTM_HEADS]
P_C = sum(MLSTM_SPLITS)

kernel_name = "hybrid_gdn_gla_mlstm_macaron_adaln_step"


def _split(x, sizes):
    return jnp.split(x, np.cumsum(sizes)[:-1].tolist(), axis=-1)


def _rmsnorm(x, g):
    xf = x.astype(jnp.float32)
    y = xf * lax.rsqrt(jnp.mean(xf * xf, axis=-1, keepdims=True) + EPS)
    return (y * g.astype(jnp.float32)).astype(x.dtype)


def _l2norm(x):
    return x * lax.rsqrt(jnp.sum(x * x, axis=-1, keepdims=True) + EPS)


def _chunk_len(T):
    L = min(CHUNK, T)
    assert T % L == 0
    return L


def _to_chunks(x, L):
    B, T, H = x.shape[:3]
    x = x.reshape((B, T // L, L, H) + x.shape[3:])
    return jnp.moveaxis(x, (1, 3), (0, 2))


def _from_chunks(y):
    y = jnp.moveaxis(y, (0, 2), (1, 3))
    B, N, L, H, d = y.shape
    return y.reshape(B, N * L, H, d)


def _masks(L):
    incl = jnp.tril(jnp.ones((L, L), dtype=bool))
    strict = jnp.tril(jnp.ones((L, L), dtype=bool), -1)
    return incl, strict


def _scan_chunks(step, init, seqs, L):
    xs = tuple(_to_chunks(s, L) for s in seqs)
    final, ys = lax.scan(step, init, xs)
    return final, _from_chunks(ys)


def _gdn_chunk(S, inp):
    q, k, v, loga, beta = inp
    incl, strict = _masks(q.shape[2])
    g = jnp.cumsum(loga, axis=-1)
    diff = g[..., :, None] - g[..., None, :]
    dec_strict = jnp.exp(jnp.where(strict, diff, -jnp.inf))
    dec_incl = jnp.exp(jnp.where(incl, diff, -jnp.inf))
    eg = jnp.exp(g)
    A = beta[..., None] * dec_strict * jnp.einsum('bhid,bhjd->bhij', k, k)
    rhs = beta[..., None] * (v - eg[..., None] * jnp.einsum('bhid,bhde->bhie', k, S))
    E = lax.linalg.triangular_solve(A, rhs, left_side=True, lower=True, unit_diagonal=True)
    qk = jnp.einsum('bhid,bhjd->bhij', q, k)
    o = eg[..., None] * jnp.einsum('bhid,bhde->bhie', q, S) + jnp.einsum('bhij,bhje->bhie', dec_incl * qk, E)
    gL = g[..., -1]
    S_new = jnp.exp(gL)[..., None, None] * S + jnp.einsum(
        'bhjd,bhje->bhde', k * jnp.exp(gL[..., None] - g)[..., None], E)
    return S_new, o


def _gla_chunk(S, inp):
    q, k, v, loga = inp
    incl, _ = _masks(q.shape[2])
    G = jnp.cumsum(loga, axis=2)
    dec = jnp.exp(jnp.where(incl[:, :, None], G[:, :, :, None, :] - G[:, :, None, :, :], -jnp.inf))
    att = jnp.einsum('bhid,bhjd,bhijd->bhij', q, k, dec)
    o = jnp.einsum('bhid,bhde->bhie', q * jnp.exp(G), S) + jnp.einsum('bhij,bhje->bhie', att, v)
    GL = G[:, :, -1:, :]
    S_new = jnp.exp(GL[:, :, 0, :])[..., None] * S + jnp.einsum('bhjd,bhje->bhde', k * jnp.exp(GL - G), v)
    return S_new, o


def _mlstm_chunk(carry, inp):
    C, n, m = carry
    q, k, v, logi, logf = inp
    incl, _ = _masks(q.shape[2])
    b = jnp.cumsum(logf, axis=-1)
    Dlog = jnp.where(incl, b[..., :, None] - b[..., None, :] + logi[..., None, :], -jnp.inf)
    inter = b + m[..., None]
    mi = jnp.maximum(inter, jnp.max(Dlog, axis=-1))
    w_inter = jnp.exp(inter - mi)
    W = jnp.exp(Dlog - mi[..., None]) * jnp.einsum('bhid,bhjd->bhij', q, k)
    num = w_inter[..., None] * jnp.einsum('bhid,bhde->bhie', q, C) + jnp.einsum('bhij,bhje->bhie', W, v)
    den = w_inter * jnp.einsum('bhid,bhd->bhi', q, n) + jnp.sum(W, axis=-1)
    h = num / jnp.maximum(jnp.abs(den), jnp.exp(-mi))[..., None]
    mL = mi[..., -1]
    wk = jnp.exp(b[..., -1:] - b + logi - mL[..., None])
    decay = jnp.exp(b[..., -1] + m - mL)
    C_new = decay[..., None, None] * C + jnp.einsum('bhjd,bhje->bhde', k * wk[..., None], v)
    n_new = decay[..., None] * n + jnp.einsum('bhjd,bhj->bhd', k, wk)
    return (C_new, n_new, mL), h


def _causal_conv(x, buf, w):
    T = x.shape[1]
    xp = jnp.concatenate([buf.astype(x.dtype), x], axis=1)
    y = xp[:, 0:T] * w[0]
    for j in range(1, GDN_CONV):
        y = y + xp[:, j:j + T] * w[j]
    return jax.nn.silu(y), xp[:, -(GDN_CONV - 1):]


def _mix_gdn_gla(h, conv_buf, S_gdn, S_gla, w_in, w_out, conv_w, A_log, dt_bias, gdn_g, gla_w2, gla_b, gla_g):
    f32 = jnp.float32
    B, T, _ = h.shape
    L = _chunk_len(T)
    proj = jnp.einsum('btd,dp->btp', h, w_in).astype(f32)
    qkv_a, g_a, a_a, b_a, q_b, k_b, v_b, r_b, lr_b = _split(proj, AB_SPLITS)
    qkv_a, new_buf = _causal_conv(qkv_a, conv_buf, conv_w.astype(f32))
    q_a, k_a, v_a = _split(qkv_a, [GDN_HEADS * GDN_DK, GDN_HEADS * GDN_DK, GDN_W])
    q_a = _l2norm(q_a.reshape(B, T, GDN_HEADS, GDN_DK)) * (GDN_DK ** -0.5)
    k_a = _l2norm(k_a.reshape(B, T, GDN_HEADS, GDN_DK))
    v_a = v_a.reshape(B, T, GDN_HEADS, GDN_DV)
    loga = -jnp.exp(A_log.astype(f32)) * jax.nn.softplus(a_a + dt_bias.astype(f32))
    beta = jax.nn.sigmoid(b_a)
    S_gdn_new, o_a = _scan_chunks(_gdn_chunk, S_gdn.astype(f32), (q_a, k_a, v_a, loga, beta), L)
    o_a = _rmsnorm(o_a, gdn_g) * jax.nn.silu(g_a).reshape(B, T, GDN_HEADS, GDN_DV)
    q_b = q_b.reshape(B, T, GLA_HEADS, GLA_DK) * (GLA_DK ** -0.5)
    k_b = k_b.reshape(B, T, GLA_HEADS, GLA_DK)
    v_b = v_b.reshape(B, T, GLA_HEADS, GLA_DV)
    loga_b = jax.nn.log_sigmoid(lr_b @ gla_w2.astype(f32) + gla_b.astype(f32)) / GLA_TAU
    loga_b = loga_b.reshape(B, T, GLA_HEADS, GLA_DK)
    S_gla_new, o_b = _scan_chunks(_gla_chunk, S_gla.astype(f32), (q_b, k_b, v_b, loga_b), L)
    o_b = _rmsnorm(o_b, gla_g) * jax.nn.silu(r_b).reshape(B, T, GLA_HEADS, GLA_DV)
    o = jnp.concatenate([o_a.reshape(B, T, GDN_W), o_b.reshape(B, T, GLA_W)], axis=-1).astype(h.dtype)
    return jnp.einsum('btw,wd->btd', o, w_out), new_buf, S_gdn_new, S_gla_new


def _mix_mlstm(h, C, n, m, w_in, w_out, gate_b, norm_g):
    f32 = jnp.float32
    B, T, _ = h.shape
    L = _chunk_len(T)
    proj = jnp.einsum('btd,dp->btp', h, w_in).astype(f32)
    q, k, v, o, ig, fg = _split(proj, MLSTM_SPLITS)
    q = q.reshape(B, T, MLSTM_HEADS, MLSTM_DK) * (MLSTM_DK ** -0.5)
    k = k.reshape(B, T, MLSTM_HEADS, MLSTM_DK)
    v = v.reshape(B, T, MLSTM_HEADS, MLSTM_DV)
    gb = gate_b.astype(f32)
    logi = ig + gb[:MLSTM_HEADS]
    logf = jax.nn.log_sigmoid(fg + gb[MLSTM_HEADS:])
    (C_new, n_new, m_new), hh = _scan_chunks(
        _mlstm_chunk, (C.astype(f32), n.astype(f32), m.astype(f32)), (q, k, v, logi, logf), L)
    hh = _rmsnorm(hh, norm_g) * jax.nn.sigmoid(o).reshape(B, T, MLSTM_HEADS, MLSTM_DV)
    out = jnp.einsum('btw,wd->btd', hh.reshape(B, T, MLSTM_W).astype(h.dtype), w_out)
    return out, C_new, n_new, m_new


def _swiglu(x, w_in, w_out):
    g, u = jnp.split(jnp.einsum('btd,df->btf', x, w_in), 2, axis=-1)
    return jnp.einsum('btf,fd->btd', jax.nn.silu(g) * u, w_out)


def _trunk(x, c, states, p):
    st_gS, st_conv, st_lS, st_C, st_n, st_m = states
    new_gS, new_conv, new_lS, new_C, new_n, new_m = [], [], [], [], [], []
    B, _, D = x.shape
    for layer in range(DEPTH):
        mod = (jax.nn.silu(c) @ p['ada_w'][layer] + p['ada_b'][layer]).reshape(B, N_SUB, 3, D)
        shift, scale, gate = mod[:, :, 0, None, :], mod[:, :, 1, None, :], mod[:, :, 2, None, :]

        def pre(s):
            return _rmsnorm(x, p['norm_g'][layer, s]) * (1 + scale[:, s]) + shift[:, s]

        x = x + 0.5 * gate[:, 0] * _swiglu(pre(0), p['ffn_w_in'][layer, 0], p['ffn_w_out'][layer, 0])
        i = layer // 2
        if layer % 2 == 0:
            y, buf, gS, lS = _mix_gdn_gla(pre(1), st_conv[i], st_gS[i], st_lS[i],
                                          p['gdn_gla_w_in'][i], p['gdn_gla_w_out'][i], p['gdn_conv_w'][i],
                                          p['gdn_A_log'][i], p['gdn_dt_bias'][i], p['gdn_norm_g'][i],
                                          p['gla_gate_w2'][i], p['gla_gate_b'][i], p['gla_norm_g'][i])
            new_gS.append(gS); new_conv.append(buf); new_lS.append(lS)
        else:
            y, Cn, nn_, mn = _mix_mlstm(pre(1), st_C[i], st_n[i], st_m[i],
                                        p['mlstm_w_in'][i], p['mlstm_w_out'][i],
                                        p['mlstm_gate_b'][i], p['mlstm_norm_g'][i])
            new_C.append(Cn); new_n.append(nn_); new_m.append(mn)
        x = x + gate[:, 1] * y
        x = x + 0.5 * gate[:, 2] * _swiglu(pre(2), p['ffn_w_in'][layer, 1], p['ffn_w_out'][layer, 1])
    y = _rmsnorm(x, p['final_norm_g'])
    return y, (jnp.stack(new_gS), jnp.stack(new_conv), jnp.stack(new_lS),
               jnp.stack(new_C), jnp.stack(new_n), jnp.stack(new_m))


def setup_inputs(seed: int = 0) -> dict:
    key = jax.random.key(seed)
    ks = jax.random.split(key, 40)
    f32 = jnp.float32

    def nrm(k, shape, s):
        return jax.random.normal(k, shape, f32) * s

    dt = jnp.exp(jax.random.uniform(ks[20], (N_EVEN, GDN_HEADS), f32, minval=float(np.log(1e-3)), maxval=float(np.log(1e-1))))
    return {
        "x_prompt": nrm(ks[0], (BATCH, SEQ, D_MODEL), 1.0),
        "x_sample": nrm(ks[1], (DEC_BATCH, DEC_SEQ, D_MODEL), 1.0),
        "state_gdn_S": nrm(ks[2], (N_EVEN, DEC_BATCH, GDN_HEADS, GDN_DK, GDN_DV), 0.3),
        "state_gdn_conv": nrm(ks[3], (N_EVEN, DEC_BATCH, GDN_CONV - 1, GDN_QKV), 1.0),
        "state_gla_S": nrm(ks[4], (N_EVEN, DEC_BATCH, GLA_HEADS, GLA_DK, GLA_DV), 1.0),
        "state_mlstm_C": nrm(ks[5], (N_ODD, DEC_BATCH, MLSTM_HEADS, MLSTM_DK, MLSTM_DV), 1.0),
        "state_mlstm_n": nrm(ks[6], (N_ODD, DEC_BATCH, MLSTM_HEADS, MLSTM_DK), 1.0),
        "state_mlstm_m": nrm(ks[7], (N_ODD, DEC_BATCH, MLSTM_HEADS), 1.0),
        "c_prompt": nrm(ks[8], (BATCH, D_MODEL), 1.0),
        "c_sample": nrm(ks[9], (DEC_BATCH, D_MODEL), 1.0),
        "ada_w": nrm(ks[10], (DEPTH, D_MODEL, N_SUB * 3 * D_MODEL), 0.5 * D_MODEL ** -0.5),
        "ada_b": nrm(ks[11], (DEPTH, N_SUB * 3 * D_MODEL), 0.02),
        "norm_g": 1.0 + nrm(ks[12], (DEPTH, N_SUB, D_MODEL), 0.02),
        "ffn_w_in": nrm(ks[13], (DEPTH, 2, D_MODEL, 2 * D_FF), D_MODEL ** -0.5),
        "ffn_w_out": nrm(ks[14], (DEPTH, 2, D_FF, D_MODEL), D_FF ** -0.5),
        "gdn_gla_w_in": nrm(ks[15], (N_EVEN, D_MODEL, P_AB), D_MODEL ** -0.5),
        "gdn_gla_w_out": nrm(ks[16], (N_EVEN, GDN_W + GLA_W, D_MODEL), (GDN_W + GLA_W) ** -0.5),
        "gdn_conv_w": nrm(ks[17], (N_EVEN, GDN_CONV, GDN_QKV), GDN_CONV ** -0.5),
        "gdn_A_log": jnp.log(jax.random.uniform(ks[18], (N_EVEN, GDN_HEADS), f32, minval=1.0, maxval=16.0)),
        "gdn_dt_bias": dt + jnp.log(-jnp.expm1(-dt)),
        "gdn_norm_g": 1.0 + nrm(ks[21], (N_EVEN, GDN_DV), 0.02),
        "gla_gate_w2": nrm(ks[22], (N_EVEN, GLA_RANK, GLA_HEADS * GLA_DK), GLA_RANK ** -0.5),
        "gla_gate_b": nrm(ks[23], (N_EVEN, GLA_HEADS * GLA_DK), 0.1),
        "gla_norm_g": 1.0 + nrm(ks[24], (N_EVEN, GLA_DV), 0.02),
        "mlstm_w_in": nrm(ks[25], (N_ODD, D_MODEL, P_C), D_MODEL ** -0.5),
        "mlstm_w_out": nrm(ks[26], (N_ODD, MLSTM_W, D_MODEL), MLSTM_W ** -0.5),
        "mlstm_gate_b": jnp.concatenate([
            nrm(ks[27], (N_ODD, MLSTM_HEADS), 0.1),
            jax.random.uniform(ks[28], (N_ODD, MLSTM_HEADS), f32, minval=3.0, maxval=6.0)], axis=-1),
        "mlstm_norm_g": 1.0 + nrm(ks[29], (N_ODD, MLSTM_DV), 0.02),
        "final_norm_g": 1.0 + nrm(ks[30], (D_MODEL,), 0.02),
    }


def reference(x_prompt, x_sample, state_gdn_S, state_gdn_conv, state_gla_S, state_mlstm_C, state_mlstm_n,
              state_mlstm_m, c_prompt, c_sample, ada_w, ada_b, norm_g, ffn_w_in, ffn_w_out, gdn_gla_w_in,
              gdn_gla_w_out, gdn_conv_w, gdn_A_log, gdn_dt_bias, gdn_norm_g, gla_gate_w2, gla_gate_b, gla_norm_g,
              mlstm_w_in, mlstm_w_out, mlstm_gate_b, mlstm_norm_g, final_norm_g):
    p = dict(ada_w=ada_w, ada_b=ada_b, norm_g=norm_g, ffn_w_in=ffn_w_in, ffn_w_out=ffn_w_out,
             gdn_gla_w_in=gdn_gla_w_in, gdn_gla_w_out=gdn_gla_w_out, gdn_conv_w=gdn_conv_w,
             gdn_A_log=gdn_A_log, gdn_dt_bias=gdn_dt_bias, gdn_norm_g=gdn_norm_g,
             gla_gate_w2=gla_gate_w2, gla_gate_b=gla_gate_b, gla_norm_g=gla_norm_g,
             mlstm_w_in=mlstm_w_in, mlstm_w_out=mlstm_w_out, mlstm_gate_b=mlstm_gate_b,
             mlstm_norm_g=mlstm_norm_g, final_norm_g=final_norm_g)
    f32 = jnp.float32
    Bp = x_prompt.shape[0]
    zero_states = (
        jnp.zeros((N_EVEN, Bp, GDN_HEADS, GDN_DK, GDN_DV), f32),
        jnp.zeros((N_EVEN, Bp, GDN_CONV - 1, GDN_QKV), f32),
        jnp.zeros((N_EVEN, Bp, GLA_HEADS, GLA_DK, GLA_DV), f32),
        jnp.zeros((N_ODD, Bp, MLSTM_HEADS, MLSTM_DK, MLSTM_DV), f32),
        jnp.zeros((N_ODD, Bp, MLSTM_HEADS, MLSTM_DK), f32),
        jnp.zeros((N_ODD, Bp, MLSTM_HEADS), f32),
    )
    y_prompt, (gS_p, gc_p, lS_p, C_p, n_p, m_p) = _trunk(x_prompt, c_prompt, zero_states, p)
    y_sample, (gS_s, gc_s, lS_s, C_s, n_s, m_s) = _trunk(
        x_sample, c_sample,
        (state_gdn_S, state_gdn_conv, state_gla_S, state_mlstm_C, state_mlstm_n, state_mlstm_m), p)
    return (y_prompt, y_sample, gS_p, gS_s, gc_p, gc_s, lS_p, lS_s, C_p, C_s, n_p, n_s, m_p, m_s)
```

```python
import functools

import numpy as np
import jax
import jax.numpy as jnp
from jax import lax
from jax.experimental import pallas as pl
from jax.experimental.pallas import tpu as pltpu

F32 = jnp.float32
BF16 = jnp.bfloat16
CHUNK = 64
EPS = 1e-6
GLA_TAU = 16.0
GLA_SUB = 16
V7X_VMEM_LIMIT = 56 * 1024 * 1024
LANE = 128
HI = lax.Precision.HIGHEST


def _pick(dim, target, align):
    best = None
    t = align
    while t <= min(dim, target):
        if dim % t == 0:
            best = t
        t += align
    return dim if best is None else best


def _round_up(x, m):
    return (x + m - 1) // m * m


def _sigmoid(x):
    return 1.0 / (1.0 + jnp.exp(-x))


def _log_sigmoid(x):
    return jnp.minimum(x, 0.0) - jnp.log1p(jnp.exp(-jnp.abs(x)))


def _softplus(x):
    return jnp.maximum(x, 0.0) + jnp.log1p(jnp.exp(-jnp.abs(x)))


def _dot(a, b):
    return jnp.dot(a.astype(BF16), b.astype(BF16), preferred_element_type=F32)


def _dot_nt(a, b):
    return lax.dot_general(a.astype(BF16), b.astype(BF16), (((1,), (1,)), ((), ())),
                           preferred_element_type=F32)


def _dot_tn(a, b):
    return lax.dot_general(a.astype(BF16), b.astype(BF16), (((0,), (0,)), ((), ())),
                           preferred_element_type=F32)


def _dot_hi(a, b):
    return jnp.dot(a, b, precision=HI, preferred_element_type=F32)


def _params(sem):
    return pltpu.CompilerParams(dimension_semantics=sem, vmem_limit_bytes=V7X_VMEM_LIMIT)


def _chunk_pos(g, Bp, NCp, NCs):
    npc = Bp * NCp
    in_p = g < npc
    r = g - npc
    seq = jnp.where(in_p, g // NCp, Bp + r // NCs)
    cin = jnp.where(in_p, g % NCp, r % NCs)
    last = jnp.where(in_p, NCp - 1, NCs - 1)
    return seq, cin == 0, cin == last


def _tri_masks(L):
    ii = lax.broadcasted_iota(jnp.int32, (L, L), 0)
    jj = lax.broadcasted_iota(jnp.int32, (L, L), 1)
    return ii, jj


def _ada_kernel(c_ref, w_ref, b_ref, o_ref):
    c = c_ref[...]
    sc = (c * _sigmoid(c)).astype(BF16)
    o_ref[0] = jnp.dot(sc, w_ref[0].astype(BF16), preferred_element_type=F32) + b_ref[0]


def _ada_mod(c, ada_w, ada_b):
    depth, D, N = ada_w.shape
    Bt = c.shape[0]
    tn = _pick(N, 512, LANE)
    return pl.pallas_call(
        _ada_kernel,
        grid=(depth, N // tn),
        in_specs=[pl.BlockSpec((Bt, D), lambda l, j: (0, 0)),
                  pl.BlockSpec((1, D, tn), lambda l, j: (l, 0, j)),
                  pl.BlockSpec((1, 1, tn), lambda l, j: (l, 0, j))],
        out_specs=pl.BlockSpec((1, Bt, tn), lambda l, j: (l, 0, j)),
        out_shape=jax.ShapeDtypeStruct((depth, Bt, N), F32),
        compiler_params=_params(("parallel", "parallel")),
    )(c, ada_w, ada_b.reshape(depth, 1, N))


def _prenorm_kernel(x_ref, g_ref, sc_ref, sh_ref, o_ref):
    x = x_ref[...]
    tr, D = x.shape
    y = (x * lax.rsqrt(jnp.mean(x * x, axis=-1, keepdims=True) + EPS)) * g_ref[...]
    y3 = y.reshape(tr // CHUNK, CHUNK, D) * (1.0 + sc_ref[...]) + sh_ref[...]
    o_ref[...] = y3.reshape(tr, D).astype(o_ref.dtype)


def _prenorm(x, g, scale_g, shift_g):
    M, D = x.shape
    tr = _pick(M, 256, CHUNK)
    ng = tr // CHUNK
    return pl.pallas_call(
        _prenorm_kernel,
        grid=(M // tr,),
        in_specs=[pl.BlockSpec((tr, D), lambda i: (i, 0)),
                  pl.BlockSpec((1, D), lambda i: (0, 0)),
                  pl.BlockSpec((ng, 1, D), lambda i: (i, 0, 0)),
                  pl.BlockSpec((ng, 1, D), lambda i: (i, 0, 0))],
        out_specs=pl.BlockSpec((tr, D), lambda i: (i, 0)),
        out_shape=jax.ShapeDtypeStruct((M, D), BF16),
        compiler_params=_params(("parallel",)),
    )(x, g.reshape(1, D), scale_g, shift_g)


def _final_norm_kernel(x_ref, g_ref, o_ref):
    x = x_ref[...]
    o_ref[...] = (x * lax.rsqrt(jnp.mean(x * x, axis=-1, keepdims=True) + EPS)) * g_ref[...]


def _final_norm(x, g, row0, nrows):
    D = x.shape[1]
    tr = _pick(int(np.gcd(nrows, row0)) if row0 else nrows, 256, 8)
    assert row0 % tr == 0 and nrows % tr == 0
    off = row0 // tr
    return pl.pallas_call(
        _final_norm_kernel,
        grid=(nrows // tr,),
        in_specs=[pl.BlockSpec((tr, D), lambda i: (i + off, 0)),
                  pl.BlockSpec((1, D), lambda i: (0, 0))],
        out_specs=pl.BlockSpec((tr, D), lambda i: (i, 0)),
        out_shape=jax.ShapeDtypeStruct((nrows, D), F32),
        compiler_params=_params(("parallel",)),
    )(x, g.reshape(1, D))


def _mm_kernel(a_ref, w_ref, o_ref):
    o_ref[...] = jnp.dot(a_ref[...], w_ref[...], preferred_element_type=F32).astype(o_ref.dtype)


def _matmul(a, w, out_dtype):
    M, K = a.shape
    N = w.shape[1]
    tm = _pick(M, 1024, CHUNK)
    tn = _pick(N, 1024, LANE)
    return pl.pallas_call(
        _mm_kernel,
        grid=(M // tm, N // tn),
        in_specs=[pl.BlockSpec((tm, K), lambda i, j: (i, 0)),
                  pl.BlockSpec((K, tn), lambda i, j: (0, j))],
        out_specs=pl.BlockSpec((tm, tn), lambda i, j: (i, j)),
        out_shape=jax.ShapeDtypeStruct((M, N), out_dtype),
        compiler_params=_params(("parallel", "parallel")),
    )(a, w)


def _mm_swiglu_kernel(a_ref, wg_ref, wu_ref, o_ref):
    a = a_ref[...]
    g = jnp.dot(a, wg_ref[...], preferred_element_type=F32)
    u = jnp.dot(a, wu_ref[...], preferred_element_type=F32)
    o_ref[...] = ((g * _sigmoid(g)) * u).astype(o_ref.dtype)


def _matmul_swiglu(a, wg, wu):
    M, K = a.shape
    N = wg.shape[1]
    tm = _pick(M, 1024, CHUNK)
    tn = _pick(N, 512, LANE)
    return pl.pallas_call(
        _mm_swiglu_kernel,
        grid=(M // tm, N // tn),
        in_specs=[pl.BlockSpec((tm, K), lambda i, j: (i, 0)),
                  pl.BlockSpec((K, tn), lambda i, j: (0, j)),
                  pl.BlockSpec((K, tn), lambda i, j: (0, j))],
        out_specs=pl.BlockSpec((tm, tn), lambda i, j: (i, j)),
        out_shape=jax.ShapeDtypeStruct((M, N), BF16),
        compiler_params=_params(("parallel", "parallel")),
    )(a, wg, wu)


def _mm_res_kernel(a_ref, w_ref, x_ref, gate_ref, o_ref, acc_ref, *, coef, nk):
    k = pl.program_id(2)

    @pl.when(k == 0)
    def _():
        acc_ref[...] = jnp.zeros_like(acc_ref)

    acc_ref[...] += jnp.dot(a_ref[...], w_ref[...], preferred_element_type=F32)

    @pl.when(k == nk - 1)
    def _():
        tm, tn = acc_ref.shape
        y = acc_ref[...].reshape(tm // CHUNK, CHUNK, tn) * (coef * gate_ref[...])
        o_ref[...] = x_ref[...] + y.reshape(tm, tn)


def _matmul_residual(a, w, x, gate_g, coef):
    M, K = a.shape
    N = w.shape[1]
    tm = _pick(M, 1024, CHUNK)
    tn = _pick(N, 1024, LANE)
    tk = _pick(K, 3072, LANE)
    nk = K // tk
    ng = tm // CHUNK
    return pl.pallas_call(
        functools.partial(_mm_res_kernel, coef=coef, nk=nk),
        grid=(M // tm, N // tn, nk),
        in_specs=[pl.BlockSpec((tm, tk), lambda i, j, k: (i, k)),
                  pl.BlockSpec((tk, tn), lambda i, j, k: (k, j)),
                  pl.BlockSpec((tm, tn), lambda i, j, k: (i, j)),
                  pl.BlockSpec((ng, 1, tn), lambda i, j, k: (i, 0, j))],
        out_specs=pl.BlockSpec((tm, tn), lambda i, j, k: (i, j)),
        out_shape=jax.ShapeDtypeStruct((M, N), F32),
        scratch_shapes=[pltpu.VMEM((tm, tn), F32)],
        compiler_params=_params(("parallel", "parallel", "arbitrary")),
    )(a, w, x, gate_g)


def _gate_proj_kernel(a_ref, w_ref, p1_ref, p2_ref, o_ref, *, kind, H):
    y = jnp.dot(a_ref[...], w_ref[...], preferred_element_type=F32)
    col = lax.broadcasted_iota(jnp.int32, y.shape, 1)
    if kind == "gdn_gla":
        loga = -jnp.exp(p1_ref[...]) * _softplus(y + p2_ref[...])
        o_ref[...] = jnp.where(col < H, loga, jnp.where(col < 2 * H, _sigmoid(y), y))
    else:
        z = y + p1_ref[...]
        o_ref[...] = jnp.where(col < H, z, _log_sigmoid(z))


def _gate_proj(a, w, p1, p2, kind, H):
    M, K = a.shape
    N = w.shape[1]
    tm = _pick(M, 1024, CHUNK)
    return pl.pallas_call(
        functools.partial(_gate_proj_kernel, kind=kind, H=H),
        grid=(M // tm,),
        in_specs=[pl.BlockSpec((tm, K), lambda i: (i, 0)),
                  pl.BlockSpec((K, N), lambda i: (0, 0)),
                  pl.BlockSpec((1, N), lambda i: (0, 0)),
                  pl.BlockSpec((1, N), lambda i: (0, 0))],
        out_specs=pl.BlockSpec((tm, N), lambda i: (i, 0)),
        out_shape=jax.ShapeDtypeStruct((M, N), F32),
        compiler_params=_params(("parallel",)),
    )(a, w, p1, p2)


def _gdn_kernel(q_ref, k_ref, v_ref, ga_ref, pq_ref, pk_ref, pv_ref, csq_ref, csk_ref, csv_ref,
                cwq_ref, cwk_ref, cwv_ref, gcol_ref, grow_ref, s0_ref, ng_ref, buf_ref,
                o_ref, sout_ref, nbq_ref, nbk_ref, nbv_ref, s_scr, cv_scr,
                *, HB, dk, dv, Bp, NCp, NCs):
    del buf_ref
    L = CHUNK
    g = pl.program_id(1)
    _, first, last = _chunk_pos(g, Bp, NCp, NCs)
    Wc = cwq_ref.shape[0]
    npv = Wc - 1

    @pl.when(first)
    def _():
        s_scr[...] = s0_ref[0]

    def conv(idx, x_ref, p_ref, cs_ref, cw_ref):
        prev = jnp.where(first, cs_ref[0], p_ref[8 - npv:8, :])
        cv_scr[idx, 8 - npv:8, :] = prev
        cv_scr[idx, 8:8 + L, :] = x_ref[...]
        w = cw_ref[...]
        y = cv_scr[idx, 8 - npv:8 - npv + L, :] * w[0:1, :]
        for j in range(1, Wc):
            y = y + cv_scr[idx, 8 - npv + j:8 - npv + j + L, :] * w[j:j + 1, :]
        return y * _sigmoid(y)

    qc = conv(0, q_ref, pq_ref, csq_ref, cwq_ref)
    kc = conv(1, k_ref, pk_ref, csk_ref, cwk_ref)
    vc = conv(2, v_ref, pv_ref, csv_ref, cwv_ref)

    ii, jj = _tri_masks(L)
    incl = jj <= ii
    strict = jj < ii
    eye = (ii == jj).astype(F32)
    gcol = gcol_ref[0]
    gc_all = _dot_hi(incl.astype(F32), gcol[:, :HB])
    gr_all = _dot_hi(grow_ref[0, 0], (ii <= jj).astype(F32))

    for j in range(HB):
        sk = slice(j * dk, (j + 1) * dk)
        sv = slice(j * dv, (j + 1) * dv)
        q, k, v = qc[:, sk], kc[:, sk], vc[:, sv]
        qn = (q * lax.rsqrt(jnp.sum(q * q, axis=-1, keepdims=True) + EPS)) * (dk ** -0.5)
        kn = k * lax.rsqrt(jnp.sum(k * k, axis=-1, keepdims=True) + EPS)
        gc = gc_all[:, j:j + 1]
        gr = gr_all[j:j + 1, :]
        beta = gcol[:, HB + j:HB + j + 1]
        dec_i = jnp.exp(jnp.where(incl, gc - gr, -jnp.inf))
        dec_s = jnp.where(strict, dec_i, 0.0)
        eg = jnp.exp(gc)
        gL = gc[L - 1:L, :]
        kb = kn.astype(BF16)
        qb = qn.astype(BF16)
        A = beta * dec_s * _dot_nt(kb, kb)
        P = -A
        T = eye + P
        n_sq = int(np.log2(L)) - 1
        for _ in range(n_sq):
            P = _dot_hi(P, P)
            T = T + _dot_hi(T, P)
        WU = _dot_hi(T, jnp.concatenate([(beta * eg) * kn, beta * v], axis=1))
        W, U = WU[:, :dk], WU[:, dk:]
        S = s_scr[j]
        Sb = S.astype(BF16)
        E = U - _dot(W, Sb)
        o = eg * _dot(qb, Sb) + _dot(dec_i * _dot_nt(qb, kb), E)
        kw = kn * jnp.exp(gL - gc)
        s_scr[j] = jnp.exp(gL) * S + _dot_tn(kw, E)
        on = (o * lax.rsqrt(jnp.mean(o * o, axis=-1, keepdims=True) + EPS)) * ng_ref[...]
        ga = ga_ref[:, sv]
        o_ref[:, sv] = (on * (ga * _sigmoid(ga))).astype(o_ref.dtype)

    @pl.when(last)
    def _():
        sout_ref[0] = s_scr[...]
        nbq_ref[0] = q_ref[L - npv:L, :]
        nbk_ref[0] = k_ref[L - npv:L, :]
        nbv_ref[0] = v_ref[L - npv:L, :]


def _gdn(proj, col_off, gates_col, gates_row, conv_all, conv_w, S0, norm_g, buf, buf_col_off,
         H, dk, dv, HB, Bp, NCp, NCs):
    M = proj.shape[0]
    NCt = M // CHUNK
    HG = H // HB
    WK, WV = HB * dk, HB * dv
    Bt = S0.shape[0]
    npv = conv_w.shape[0] - 1
    oq, ok, ov, og = col_off
    pos = functools.partial(_chunk_pos, Bp=Bp, NCp=NCp, NCs=NCs)

    def blk(off, w):
        return pl.BlockSpec((CHUNK, w), lambda h, g: (g, off // w + h))

    def prev(off, w):
        return pl.BlockSpec((8, w), lambda h, g: (jnp.maximum(g * (CHUNK // 8) - 1, 0), off // w + h))

    def cst(off, w):
        return pl.BlockSpec((1, npv, w), lambda h, g: (pos(g)[0], 0, off // w + h))

    def cw(off, w):
        return pl.BlockSpec((conv_w.shape[0], w), lambda h, g: (0, off // w + h))

    HKall = H * dk
    in_specs = [blk(oq, WK), blk(ok, WK), blk(ov, WV), blk(og, WV),
                prev(oq, WK), prev(ok, WK), prev(ov, WV),
                cst(0, WK), cst(HKall, WK), cst(2 * HKall, WV),
                cw(0, WK), cw(HKall, WK), cw(2 * HKall, WV),
                pl.BlockSpec((1, CHUNK, 2 * HB), lambda h, g: (h, g, 0)),
                pl.BlockSpec((1, 1, HB, CHUNK), lambda h, g: (g, h, 0, 0)),
                pl.BlockSpec((1, HB, dk, dv), lambda h, g: (pos(g)[0], h, 0, 0)),
                pl.BlockSpec((1, dv), lambda h, g: (0, 0)),
                pl.BlockSpec(memory_space=pl.ANY)]
    out_specs = [pl.BlockSpec((CHUNK, WV), lambda h, g: (g, buf_col_off // WV + h)),
                 pl.BlockSpec((1, HB, dk, dv), lambda h, g: (pos(g)[0], h, 0, 0)),
                 pl.BlockSpec((1, npv, WK), lambda h, g: (pos(g)[0], 0, h)),
                 pl.BlockSpec((1, npv, WK), lambda h, g: (pos(g)[0], 0, h)),
                 pl.BlockSpec((1, npv, WV), lambda h, g: (pos(g)[0], 0, h))]
    out_shape = [jax.ShapeDtypeStruct(buf.shape, buf.dtype),
                 jax.ShapeDtypeStruct(S0.shape, F32),
                 jax.ShapeDtypeStruct((Bt, npv, H * dk), F32),
                 jax.ShapeDtypeStruct((Bt, npv, H * dk), F32),
                 jax.ShapeDtypeStruct((Bt, npv, H * dv), F32)]
    return pl.pallas_call(
        functools.partial(_gdn_kernel, HB=HB, dk=dk, dv=dv, Bp=Bp, NCp=NCp, NCs=NCs),
        grid=(HG, NCt),
        in_specs=in_specs,
        out_specs=out_specs,
        out_shape=out_shape,
        scratch_shapes=[pltpu.VMEM((HB, dk, dv), F32),
                        pltpu.VMEM((3, 8 + CHUNK, max(WK, WV)), F32)],
        input_output_aliases={17: 0},
        compiler_params=_params(("parallel", "arbitrary")),
    )(proj, proj, proj, proj, proj, proj, proj, conv_all, conv_all, conv_all,
      conv_w, conv_w, conv_w, gates_col, gates_row, S0, norm_g.reshape(1, dv), buf)


def _gla_kernel(q_ref, k_ref, v_ref, r_ref, lr_ref, w2_ref, gb_ref, s0_ref, ng_ref, buf_ref,
                o_ref, sout_ref, s_scr, *, HB, dk, dv, Bp, NCp, NCs):
    del buf_ref
    L = CHUNK
    g = pl.program_id(1)
    _, first, last = _chunk_pos(g, Bp, NCp, NCs)

    @pl.when(first)
    def _():
        s_scr[...] = s0_ref[0]

    ii, jj = _tri_masks(L)
    incl = jj <= ii
    tri = incl.astype(F32)
    krow = lax.broadcasted_iota(jnp.int32, (L, dk), 0)
    di = lax.broadcasted_iota(jnp.int32, (dk, dk), 0)
    dj = lax.broadcasted_iota(jnp.int32, (dk, dk), 1)
    lr = lr_ref[...]

    for j in range(HB):
        sk = slice(j * dk, (j + 1) * dk)
        sv = slice(j * dv, (j + 1) * dv)
        x = _dot(lr, w2_ref[:, sk]) + gb_ref[:, sk]
        G = _dot_hi(tri, _log_sigmoid(x) / GLA_TAU)
        q = q_ref[:, sk] * (dk ** -0.5)
        k = k_ref[:, sk]
        v = v_ref[:, sv]
        rows = []
        for b in range(L // GLA_SUB):
            lo, hi = b * GLA_SUB, (b + 1) * GLA_SUB
            ref_g = G[lo:lo + 1, :]
            qg = q[lo:hi, :] * jnp.exp(G[lo:hi, :] - ref_g)
            kg = k * jnp.exp(jnp.where(krow < hi, ref_g - G, -jnp.inf))
            rows.append(_dot_nt(qg, kg))
        att = jnp.where(incl, jnp.concatenate(rows, axis=0), 0.0)
        S = s_scr[j]
        o = _dot(q * jnp.exp(G), S) + _dot(att, v)
        GL = G[L - 1:L, :]
        kS = k * jnp.exp(GL - G)
        a_col = jnp.sum(jnp.where(di == dj, jnp.broadcast_to(jnp.exp(GL), (dk, dk)), 0.0),
                        axis=1, keepdims=True)
        s_scr[j] = a_col * S + _dot_tn(kS, v)
        on = (o * lax.rsqrt(jnp.mean(o * o, axis=-1, keepdims=True) + EPS)) * ng_ref[...]
        r = r_ref[:, sv]
        o_ref[:, sv] = (on * (r * _sigmoid(r))).astype(o_ref.dtype)

    @pl.when(last)
    def _():
        sout_ref[0] = s_scr[...]


def _gla(proj, col_off, lr, w2, gb, S0, norm_g, buf, buf_col_off, H, dk, dv, HB, Bp, NCp, NCs):
    M = proj.shape[0]
    NCt = M // CHUNK
    HG = H // HB
    WK, WV = HB * dk, HB * dv
    R = lr.shape[1]
    oq, ok, ov, orr = col_off
    pos = functools.partial(_chunk_pos, Bp=Bp, NCp=NCp, NCs=NCs)

    def blk(off, w):
        return pl.BlockSpec((CHUNK, w), lambda h, g: (g, off // w + h))

    in_specs = [blk(oq, WK), blk(ok, WK), blk(ov, WV), blk(orr, WV),
                pl.BlockSpec((CHUNK, R), lambda h, g: (g, 0)),
                pl.BlockSpec((R, WK), lambda h, g: (0, h)),
                pl.BlockSpec((1, WK), lambda h, g: (0, h)),
                pl.BlockSpec((1, HB, dk, dv), lambda h, g: (pos(g)[0], h, 0, 0)),
                pl.BlockSpec((1, dv), lambda h, g: (0, 0)),
                pl.BlockSpec(memory_space=pl.ANY)]
    out_specs = [pl.BlockSpec((CHUNK, WV), lambda h, g: (g, buf_col_off // WV + h)),
                 pl.BlockSpec((1, HB, dk, dv), lambda h, g: (pos(g)[0], h, 0, 0))]
    out_shape = [jax.ShapeDtypeStruct(buf.shape, buf.dtype),
                 jax.ShapeDtypeStruct(S0.shape, F32)]
    return pl.pallas_call(
        functools.partial(_gla_kernel, HB=HB, dk=dk, dv=dv, Bp=Bp, NCp=NCp, NCs=NCs),
        grid=(HG, NCt),
        in_specs=in_specs,
        out_specs=out_specs,
        out_shape=out_shape,
        scratch_shapes=[pltpu.VMEM((HB, dk, dv), F32)],
        input_output_aliases={9: 0},
        compiler_params=_params(("parallel", "arbitrary")),
    )(proj, proj, proj, proj, lr, w2, gb.reshape(1, -1), S0, norm_g.reshape(1, dv), buf)


def _mlstm_kernel(q_ref, k_ref, v_ref, og_ref, gcol_ref, grow_ref, c0_ref, n0_ref, m0_ref, ng_ref,
                  o_ref, cout_ref, nout_ref, mout_ref, c_scr, n_scr, m_scr,
                  *, HB, dk, dv, Bp, NCp, NCs):
    L = CHUNK
    g = pl.program_id(1)
    _, first, last = _chunk_pos(g, Bp, NCp, NCs)

    @pl.when(first)
    def _():
        c_scr[...] = c0_ref[0]
        n_scr[...] = n0_ref[0, 0]
        m_scr[...] = m0_ref[0, 0]

    ii, jj = _tri_masks(L)
    incl = jj <= ii
    gcol = gcol_ref[0]
    grow = grow_ref[0, 0]
    bc_all = _dot_hi(incl.astype(F32), gcol[:, HB:])
    br_all = _dot_hi(grow[HB:, :], (ii <= jj).astype(F32))

    for j in range(HB):
        sk = slice(j * dk, (j + 1) * dk)
        sv = slice(j * dv, (j + 1) * dv)
        q = q_ref[:, sk] * (dk ** -0.5)
        k = k_ref[:, sk]
        v = v_ref[:, sv]
        bc = bc_all[:, j:j + 1]
        br = br_all[j:j + 1, :]
        li_c = gcol[:, j:j + 1]
        li_r = grow[j:j + 1, :]
        m_prev = m_scr[j:j + 1, :]
        Dlog = jnp.where(incl, bc - br + li_r, -jnp.inf)
        inter = bc + m_prev
        mi = jnp.maximum(inter, jnp.max(Dlog, axis=-1, keepdims=True))
        w_inter = jnp.exp(inter - mi)
        qb = q.astype(BF16)
        Wm = jnp.exp(Dlog - mi) * _dot_nt(qb, k)
        C = c_scr[j]
        n = n_scr[j:j + 1, :]
        num = w_inter * _dot(qb, C) + _dot(Wm, v)
        den = w_inter * jnp.sum(q * n, axis=-1, keepdims=True) + jnp.sum(Wm, axis=-1, keepdims=True)
        h = num / jnp.maximum(jnp.abs(den), jnp.exp(-mi))
        mL = mi[L - 1:L, :]
        bL = bc[L - 1:L, :]
        wk = jnp.exp(bL - bc + li_c - mL)
        decay = jnp.exp(bL + m_prev - mL)
        kw = k * wk
        c_scr[j] = decay * C + _dot_tn(kw, v)
        n_scr[j:j + 1, :] = decay * n + jnp.sum(kw, axis=0, keepdims=True)
        m_scr[j:j + 1, :] = mL
        hn = (h * lax.rsqrt(jnp.mean(h * h, axis=-1, keepdims=True) + EPS)) * ng_ref[...]
        o_ref[:, sv] = (hn * _sigmoid(og_ref[:, sv])).astype(o_ref.dtype)

    @pl.when(last)
    def _():
        cout_ref[0] = c_scr[...]
        nout_ref[0, 0] = n_scr[...]
        mout_ref[0, 0] = m_scr[...]


def _mlstm(proj, col_off, gates_col, gates_row, C0, n0, m0, norm_g, H, dk, dv, HB, Bp, NCp, NCs):
    M = proj.shape[0]
    NCt = M // CHUNK
    HG = H // HB
    WK, WV = HB * dk, HB * dv
    Bt = C0.shape[0]
    oq, ok, ov, oo = col_off
    pos = functools.partial(_chunk_pos, Bp=Bp, NCp=NCp, NCs=NCs)

    def blk(off, w):
        return pl.BlockSpec((CHUNK, w), lambda h, g: (g, off // w + h))

    in_specs = [blk(oq, WK), blk(ok, WK), blk(ov, WV), blk(oo, WV),
                pl.BlockSpec((1, CHUNK, 2 * HB), lambda h, g: (h, g, 0)),
                pl.BlockSpec((1, 1, 2 * HB, CHUNK), lambda h, g: (g, h, 0, 0)),
                pl.BlockSpec((1, HB, dk, dv), lambda h, g: (pos(g)[0], h, 0, 0)),
                pl.BlockSpec((1, 1, HB, dk), lambda h, g: (pos(g)[0], h, 0, 0)),
                pl.BlockSpec((1, 1, HB, 1), lambda h, g: (pos(g)[0], h, 0, 0)),
                pl.BlockSpec((1, dv), lambda h, g: (0, 0))]
    out_specs = [pl.BlockSpec((CHUNK, WV), lambda h, g: (g, h)),
                 pl.BlockSpec((1, HB, dk, dv), lambda h, g: (pos(g)[0], h, 0, 0)),
                 pl.BlockSpec((1, 1, HB, dk), lambda h, g: (pos(g)[0], h, 0, 0)),
                 pl.BlockSpec((1, 1, HB, 1), lambda h, g: (pos(g)[0], h, 0, 0))]
    out_shape = [jax.ShapeDtypeStruct((M, H * dv), BF16),
                 jax.ShapeDtypeStruct(C0.shape, F32),
                 jax.ShapeDtypeStruct((Bt, HG, HB, dk), F32),
                 jax.ShapeDtypeStruct((Bt, HG, HB, 1), F32)]
    return pl.pallas_call(
        functools.partial(_mlstm_kernel, HB=HB, dk=dk, dv=dv, Bp=Bp, NCp=NCp, NCs=NCs),
        grid=(HG, NCt),
        in_specs=in_specs,
        out_specs=out_specs,
        out_shape=out_shape,
        scratch_shapes=[pltpu.VMEM((HB, dk, dv), F32),
                        pltpu.VMEM((HB, dk), F32),
                        pltpu.VMEM((HB, 1), F32)],
        compiler_params=_params(("parallel", "arbitrary")),
    )(proj, proj, proj, proj, gates_col, gates_row, C0,
      n0.reshape(Bt, HG, HB, dk), m0.reshape(Bt, HG, HB, 1), norm_g.reshape(1, dv))


def _gates_col(g, n_kinds, HG, HB):
    M = g.shape[0]
    return g.reshape(M, n_kinds, HG, HB).transpose(2, 0, 1, 3).reshape(HG, M, n_kinds * HB)


def _gates_row(g, n_kinds, HG, HB):
    M = g.shape[0]
    g = g.reshape(M // CHUNK, CHUNK, n_kinds, HG, HB).transpose(0, 3, 2, 4, 1)
    return g.reshape(M // CHUNK, HG, n_kinds * HB, CHUNK)


def _with_zero_prompt(state, Bp):
    return jnp.concatenate([jnp.zeros((Bp,) + state.shape[1:], F32), state.astype(F32)], axis=0)


def kernel(x_prompt, x_sample, state_gdn_S, state_gdn_conv, state_gla_S, state_mlstm_C, state_mlstm_n,
           state_mlstm_m, c_prompt, c_sample, ada_w, ada_b, norm_g, ffn_w_in, ffn_w_out, gdn_gla_w_in,
           gdn_gla_w_out, gdn_conv_w, gdn_A_log, gdn_dt_bias, gdn_norm_g, gla_gate_w2, gla_gate_b,
           gla_norm_g, mlstm_w_in, mlstm_w_out, mlstm_gate_b, mlstm_norm_g, final_norm_g):
    Bp, Tp, D = x_prompt.shape
    Bs, Ts, _ = x_sample.shape
    depth = ada_w.shape[0]
    n_sub = norm_g.shape[1]
    assert Tp % CHUNK == 0 and Ts % CHUNK == 0
    NCp, NCs = Tp // CHUNK, Ts // CHUNK
    Mp, Ms = Bp * Tp, Bs * Ts
    Bt = Bp + Bs
    F = ffn_w_out.shape[2]
    Fp = _round_up(F, 512)

    _, _, Hg, dkg, dvg = state_gdn_S.shape
    _, _, Hl, dkl, dvl = state_gla_S.shape
    _, _, Hm, dkm, dvm = state_mlstm_C.shape
    Wg, Wl, Wm = Hg * dvg, Hl * dvl, Hm * dvm
    R = gla_gate_w2.shape[1]
    assert state_gdn_conv.shape[2] + 1 == gdn_conv_w.shape[1] and state_gdn_conv.shape[2] <= 8
    HBg = min(4, Hg)
    HBl = 1
    HBm = min(2, Hm)

    x = jnp.concatenate([x_prompt.reshape(Mp, D), x_sample.reshape(Ms, D)], axis=0)
    c = jnp.concatenate([c_prompt, c_sample], axis=0)

    mod = _ada_mod(c, ada_w, ada_b)
    seq_of_group = np.concatenate([np.repeat(np.arange(Bp), NCp), Bp + np.repeat(np.arange(Bs), NCs)])
    modg = mod.reshape(depth, Bt, n_sub, 3, D)[:, seq_of_group]

    def mod_of(layer, s, kind):
        return modg[layer, :, s, kind][:, None, :]

    def ffn(x, layer, f, s):
        w_in = ffn_w_in[layer, f]
        pad = ((0, 0), (0, Fp - F))
        wg = jnp.pad(w_in[:, :F], pad).astype(BF16)
        wu = jnp.pad(w_in[:, F:], pad).astype(BF16)
        wo = jnp.pad(ffn_w_out[layer, f], ((0, Fp - F), (0, 0))).astype(BF16)
        h = _prenorm(x, norm_g[layer, s], mod_of(layer, s, 1), mod_of(layer, s, 0))
        hh = _matmul_swiglu(h, wg, wu)
        return _matmul_residual(hh, wo, x, mod_of(layer, s, 2), 0.5)

    new_gS, new_conv, new_lS, new_C, new_n, new_m = [], [], [], [], [], []
    for layer in range(depth):
        x = ffn(x, layer, 0, 0)
        i = layer // 2
        h = _prenorm(x, norm_g[layer, 1], mod_of(layer, 1, 1), mod_of(layer, 1, 0))
        if layer % 2 == 0:
            w = gdn_gla_w_in[i]
            nqk = Hg * dkg
            splits = np.cumsum([2 * nqk + Wg, Wg, Hg, Hg, Hl * dkl, Hl * dkl, Wl, Wl, R])
            qkv_a, g_a, a_a, b_a, q_b, k_b, v_b, r_b, lr_b = jnp.split(w, splits[:-1].tolist(), axis=1)
            w_main = jnp.concatenate([qkv_a, g_a, q_b, k_b, v_b, r_b], axis=1).astype(BF16)
            w_gate = jnp.concatenate([a_a, b_a, lr_b], axis=1).astype(BF16)
            ng = w_gate.shape[1]
            p1 = jnp.zeros((1, ng), F32).at[0, :Hg].set(gdn_A_log[i].astype(F32))
            p2 = jnp.zeros((1, ng), F32).at[0, :Hg].set(gdn_dt_bias[i].astype(F32))
            proj = _matmul(h, w_main, F32)
            gates = _gate_proj(h, w_gate, p1, p2, "gdn_gla", Hg)
            HGg = Hg // HBg
            gcol = _gates_col(gates[:, :2 * Hg], 2, HGg, HBg)
            grow = _gates_row(gates[:, :Hg], 1, HGg, HBg)
            lr = gates[:, 2 * Hg:]
            o_q, o_k, o_v = 0, nqk, 2 * nqk
            o_g = 2 * nqk + Wg
            o_qb = o_g + Wg
            o_kb = o_qb + Hl * dkl
            o_vb = o_kb + Hl * dkl
            o_rb = o_vb + Wl
            buf = jnp.zeros((Mp + Ms, Wg + Wl), BF16)
            buf, gS, nbq, nbk, nbv = _gdn(
                proj, (o_q, o_k, o_v, o_g), gcol, grow,
                _with_zero_prompt(state_gdn_conv[i], Bp), gdn_conv_w[i].astype(F32),
                _with_zero_prompt(state_gdn_S[i], Bp), gdn_norm_g[i].astype(F32), buf, 0,
                Hg, dkg, dvg, HBg, Bp, NCp, NCs)
            buf, lS = _gla(
                proj, (o_qb, o_kb, o_vb, o_rb), lr, gla_gate_w2[i].astype(F32),
                gla_gate_b[i].astype(F32), _with_zero_prompt(state_gla_S[i], Bp),
                gla_norm_g[i].astype(F32), buf, Wg, Hl, dkl, dvl, HBl, Bp, NCp, NCs)
            new_gS.append(gS)
            new_conv.append(jnp.concatenate([nbq, nbk, nbv], axis=-1))
            new_lS.append(lS)
            w_out = gdn_gla_w_out[i].astype(BF16)
        else:
            w = mlstm_w_in[i]
            nqk = Hm * dkm
            w_main = w[:, :2 * nqk + 2 * Wm].astype(BF16)
            w_gate = w[:, 2 * nqk + 2 * Wm:].astype(BF16)
            gb = mlstm_gate_b[i].astype(F32).reshape(1, 2 * Hm)
            proj = _matmul(h, w_main, F32)
            gates = _gate_proj(h, w_gate, gb, gb, "mlstm", Hm)
            HGm = Hm // HBm
            gcol = _gates_col(gates, 2, HGm, HBm)
            grow = _gates_row(gates, 2, HGm, HBm)
            buf, Cn, nn, mn = _mlstm(
                proj, (0, nqk, 2 * nqk, 2 * nqk + Wm), gcol, grow,
                _with_zero_prompt(state_mlstm_C[i], Bp), _with_zero_prompt(state_mlstm_n[i], Bp),
                _with_zero_prompt(state_mlstm_m[i], Bp), mlstm_norm_g[i].astype(F32),
                Hm, dkm, dvm, HBm, Bp, NCp, NCs)
            new_C.append(Cn)
            new_n.append(nn.reshape(Bt, Hm, dkm))
            new_m.append(mn.reshape(Bt, Hm))
            w_out = mlstm_w_out[i].astype(BF16)
        x = _matmul_residual(buf, w_out, x, mod_of(layer, 1, 2), 1.0)
        x = ffn(x, layer, 1, 2)

    y_p = _final_norm(x, final_norm_g, 0, Mp).reshape(Bp, Tp, D)
    y_s = _final_norm(x, final_norm_g, Mp, Ms).reshape(Bs, Ts, D)

    def split(states):
        st = jnp.stack(states)
        return st[:, :Bp], st[:, Bp:]

    gS_p, gS_s = split(new_gS)
    gc_p, gc_s = split(new_conv)
    lS_p, lS_s = split(new_lS)
    C_p, C_s = split(new_C)
    n_p, n_s = split(new_n)
    m_p, m_s = split(new_m)
    return (y_p, y_s, gS_p, gS_s, gc_p, gc_s, lS_p, lS_s, C_p, C_s, n_p, n_s, m_p, m_s)
```

```python
import functools

import numpy as np
import jax
import jax.numpy as jnp
from jax import lax
from jax.experimental import pallas as pl
from jax.experimental.pallas import tpu as pltpu

F32 = jnp.float32
BF16 = jnp.bfloat16
CHUNK = 64
EPS = 1e-6
GLA_TAU = 16.0
GLA_SUB = 16
V7X_VMEM_LIMIT = 58 * 1024 * 1024
LANE = 128
SUBLANE = 8
HI = lax.Precision.HIGHEST


def _pick(dim, target, align):
    best = None
    t = align
    while t <= min(dim, target):
        if dim % t == 0:
            best = t
        t += align
    return dim if best is None else best


def _sigmoid(x):
    return 1.0 / (1.0 + jnp.exp(-x))


def _log_sigmoid(x):
    return jnp.minimum(x, 0.0) - jnp.log1p(jnp.exp(-jnp.abs(x)))


def _softplus(x):
    return jnp.maximum(x, 0.0) + jnp.log1p(jnp.exp(-jnp.abs(x)))


def _dot(a, b):
    return jnp.dot(a.astype(BF16), b.astype(BF16), preferred_element_type=F32)


def _dot_nt(a, b):
    return lax.dot_general(a.astype(BF16), b.astype(BF16), (((1,), (1,)), ((), ())),
                           preferred_element_type=F32)


def _dot_tn(a, b):
    return lax.dot_general(a.astype(BF16), b.astype(BF16), (((0,), (0,)), ((), ())),
                           preferred_element_type=F32)


def _dot_hi(a, b):
    return jnp.dot(a, b, precision=HI, preferred_element_type=F32)


def _split_bf16(a):
    hi = a.astype(BF16)
    return hi, (a - hi.astype(F32)).astype(BF16)


def _dot_x3(a, b):
    (ah, al), (bh, bl) = a, b
    d = functools.partial(jnp.dot, preferred_element_type=F32)
    return d(ah, bh) + (d(ah, bl) + d(al, bh))


def _params(sem):
    return pltpu.CompilerParams(dimension_semantics=sem, vmem_limit_bytes=V7X_VMEM_LIMIT)


def _chunk_pos(g, Bp, NCp, NCs):
    npc = Bp * NCp
    in_p = g < npc
    r = g - npc
    seq = jnp.where(in_p, g // NCp, Bp + r // NCs)
    cin = jnp.where(in_p, g % NCp, r % NCs)
    last = jnp.where(in_p, NCp - 1, NCs - 1)
    return seq, cin == 0, cin == last, in_p


def _tri_masks(L):
    ii = lax.broadcasted_iota(jnp.int32, (L, L), 0)
    jj = lax.broadcasted_iota(jnp.int32, (L, L), 1)
    return ii, jj


class _Seqs:
    def __init__(self, Bp, NCp, Bs, NCs):
        self.Bp, self.NCp, self.Bs, self.NCs = Bp, NCp, Bs, NCs
        self.Mp, self.Ms = Bp * NCp * CHUNK, Bs * NCs * CHUNK
        self.M = self.Mp + self.Ms
        self.NG = self.M // CHUNK
        self.Mg = int(np.gcd(self.Mp, self.Ms))

    def pos(self, g):
        return _chunk_pos(g, self.Bp, self.NCp, self.NCs)

    def seq_p(self, g):
        return jnp.minimum(self.pos(g)[0], self.Bp - 1)

    def seq_s(self, g):
        return jnp.maximum(self.pos(g)[0] - self.Bp, 0)


def _x_specs(x, tm, tn, sq, col_of):
    if isinstance(x, tuple):
        npt = sq.Mp // tm
        return [pl.BlockSpec((tm, tn), lambda i, *r: (jnp.minimum(i, npt - 1), col_of(*r))),
                pl.BlockSpec((tm, tn), lambda i, *r: (jnp.maximum(i - npt, 0), col_of(*r)))], list(x), npt
    return [pl.BlockSpec((tm, tn), lambda i, *r: (i, col_of(*r)))], [x], 0


def _on_source(x_refs, npt, body):
    if len(x_refs) == 1:
        body(x_refs[0])
    else:
        i = pl.program_id(0)
        pl.when(i < npt)(lambda: body(x_refs[0]))
        pl.when(i >= npt)(lambda: body(x_refs[1]))


def _ada_kernel(c_ref, w_ref, b_ref, o_ref, res_scr, *, sq):
    c = c_ref[...]
    sc = (c * _sigmoid(c)).astype(BF16)
    res_scr[...] = jnp.dot(sc, w_ref[0].astype(BF16), preferred_element_type=F32) + b_ref[0]
    tn = res_scr.shape[1]
    for b in range(sq.Bp):
        o_ref[0, b * sq.NCp:(b + 1) * sq.NCp, :] = jnp.broadcast_to(res_scr[b:b + 1, :], (sq.NCp, tn))
    base = sq.Bp * sq.NCp
    if sq.NCs == 1:
        o_ref[0, base:base + sq.Bs, :] = res_scr[sq.Bp:sq.Bp + sq.Bs, :]
    else:
        for b in range(sq.Bs):
            o_ref[0, base + b * sq.NCs:base + (b + 1) * sq.NCs, :] = jnp.broadcast_to(
                res_scr[sq.Bp + b:sq.Bp + b + 1, :], (sq.NCs, tn))


def _ada_mod(c, ada_w, ada_b, sq):
    depth, D, N = ada_w.shape
    Bt = c.shape[0]
    tn = _pick(N, 512, LANE)
    return pl.pallas_call(
        functools.partial(_ada_kernel, sq=sq),
        grid=(depth, N // tn),
        in_specs=[pl.BlockSpec((Bt, D), lambda l, j: (0, 0)),
                  pl.BlockSpec((1, D, tn), lambda l, j: (l, 0, j)),
                  pl.BlockSpec((1, 1, tn), lambda l, j: (l, 0, j))],
        out_specs=pl.BlockSpec((1, sq.NG, tn), lambda l, j: (l, 0, j)),
        out_shape=jax.ShapeDtypeStruct((depth, sq.NG, N), F32),
        scratch_shapes=[pltpu.VMEM((Bt, tn), F32)],
        compiler_params=_params(("parallel", "parallel")),
    )(c, ada_w, ada_b.reshape(depth, 1, N))


def _row_rms_scale(x_ref, rows):
    D = x_ref.shape[1]
    acc = None
    for c in range(0, D, LANE):
        xc = x_ref[rows, c:c + LANE]
        acc = xc * xc if acc is None else acc + xc * xc
    return lax.rsqrt(jnp.sum(acc, axis=-1, keepdims=True) * (1.0 / D) + EPS)


def _prenorm_kernel(*refs, npt):
    *x_refs, g_ref, sc_ref, sh_ref, o_ref = refs

    def body(x_ref):
        D = x_ref.shape[1]
        cw = _pick(D, 512, LANE)
        for r in range(x_ref.shape[0] // CHUNK):
            rows = slice(r * CHUNK, (r + 1) * CHUNK)
            rinv = _row_rms_scale(x_ref, rows)
            for c in range(0, D, cw):
                cols = slice(c, c + cw)
                gm = g_ref[:, cols] * (1.0 + sc_ref[r:r + 1, cols])
                o_ref[rows, cols] = ((x_ref[rows, cols] * rinv) * gm + sh_ref[r:r + 1, cols]).astype(o_ref.dtype)

    _on_source(x_refs, npt, body)


def _prenorm(x, g, modg, layer, s, sq):
    D = g.shape[0]
    tr = _pick(sq.Mg, 512, CHUNK * SUBLANE)
    ng = tr // CHUNK
    x_specs, xs, npt = _x_specs(x, tr, D, sq, lambda: 0)
    return pl.pallas_call(
        functools.partial(_prenorm_kernel, npt=npt),
        grid=(sq.M // tr,),
        in_specs=x_specs + [pl.BlockSpec((1, D), lambda i: (0, 0)),
                            pl.BlockSpec((None, ng, D), lambda i: (layer, i, s * 3 + 1)),
                            pl.BlockSpec((None, ng, D), lambda i: (layer, i, s * 3))],
        out_specs=pl.BlockSpec((tr, D), lambda i: (i, 0)),
        out_shape=jax.ShapeDtypeStruct((sq.M, D), BF16),
        compiler_params=_params(("parallel",)),
    )(*xs, g.reshape(1, D), modg, modg)


def _final_norm_kernel(x_ref, g_ref, o_ref):
    D = x_ref.shape[1]
    cw = _pick(D, 512, LANE)
    rt = _pick(x_ref.shape[0], CHUNK, SUBLANE)
    for r in range(0, x_ref.shape[0], rt):
        rows = slice(r, r + rt)
        rinv = _row_rms_scale(x_ref, rows)
        for c in range(0, D, cw):
            cols = slice(c, c + cw)
            o_ref[rows, cols] = (x_ref[rows, cols] * rinv) * g_ref[:, cols]


def _final_norm(x, g, row0, nrows):
    D = x.shape[1]
    tr = _pick(int(np.gcd(nrows, row0)) if row0 else nrows, 256, SUBLANE)
    assert row0 % tr == 0 and nrows % tr == 0
    off = row0 // tr
    return pl.pallas_call(
        _final_norm_kernel,
        grid=(nrows // tr,),
        in_specs=[pl.BlockSpec((tr, D), lambda i: (i + off, 0)),
                  pl.BlockSpec((1, D), lambda i: (0, 0))],
        out_specs=pl.BlockSpec((tr, D), lambda i: (i, 0)),
        out_shape=jax.ShapeDtypeStruct((nrows, D), F32),
        compiler_params=_params(("parallel",)),
    )(x, g.reshape(1, D))


def _w_spec(w, lead, K, tn, col_of):
    return pl.BlockSpec((None,) * len(lead) + (K, tn), lambda i, j: tuple(lead) + (0, col_of(j)))


def _mm_kernel(a_ref, w_ref, o_ref):
    o_ref[...] = jnp.dot(a_ref[...], w_ref[...], preferred_element_type=F32).astype(o_ref.dtype)


def _matmul(a, w, lead, ncols, out_dtype):
    M, K = a.shape
    tm = _pick(M, 1024, CHUNK)
    tn = _pick(ncols, 1024, LANE)
    return pl.pallas_call(
        _mm_kernel,
        grid=(M // tm, ncols // tn),
        in_specs=[pl.BlockSpec((tm, K), lambda i, j: (i, 0)),
                  _w_spec(w, lead, K, tn, lambda j: j)],
        out_specs=pl.BlockSpec((tm, tn), lambda i, j: (i, j)),
        out_shape=jax.ShapeDtypeStruct((M, ncols), out_dtype),
        compiler_params=_params(("parallel", "parallel")),
    )(a, w)


def _mm_swiglu_kernel(a_ref, wg_ref, wu_ref, o_ref):
    a = a_ref[...]
    g = jnp.dot(a, wg_ref[...].astype(BF16), preferred_element_type=F32)
    u = jnp.dot(a, wu_ref[...].astype(BF16), preferred_element_type=F32)
    rt = _pick(g.shape[0], 256, SUBLANE)
    for r in range(0, g.shape[0], rt):
        gr = g[r:r + rt, :]
        o_ref[r:r + rt, :] = ((gr * _sigmoid(gr)) * u[r:r + rt, :]).astype(o_ref.dtype)


def _matmul_swiglu(a, w, lead):
    M, K = a.shape
    F = w.shape[-1] // 2
    tm = _pick(M, 2048, CHUNK)
    tn = _pick(F, 256, LANE)
    nf = F // tn
    return pl.pallas_call(
        _mm_swiglu_kernel,
        grid=(M // tm, nf),
        in_specs=[pl.BlockSpec((tm, K), lambda i, j: (i, 0), pipeline_mode=pl.Buffered(1)),
                  _w_spec(w, lead, K, tn, lambda j: j),
                  _w_spec(w, lead, K, tn, lambda j: nf + j)],
        out_specs=pl.BlockSpec((tm, tn), lambda i, j: (i, j)),
        out_shape=jax.ShapeDtypeStruct((M, F), BF16),
        compiler_params=_params(("parallel", "parallel")),
    )(a, w, w)


def _mm_res_kernel(a_ref, w_ref, *refs, coef, npt):
    *x_refs, gate_ref, o_ref = refs
    y = jnp.dot(a_ref[...], w_ref[...], preferred_element_type=F32)

    def body(x_ref):
        for r in range(y.shape[0] // CHUNK):
            rows = slice(r * CHUNK, (r + 1) * CHUNK)
            o_ref[rows, :] = x_ref[rows, :] + (coef * gate_ref[r:r + 1, :]) * y[rows, :]

    _on_source(x_refs, npt, body)


def _matmul_residual(a, w, lead, x, modg, layer, s, coef, sq, tile):
    M, K = a.shape
    N = w.shape[-1]
    tm = _pick(sq.Mg, tile, CHUNK * SUBLANE)
    tn = _pick(N, tile, LANE)
    ng = tm // CHUNK
    gcol = (s * 3 + 2) * (N // tn)
    x_specs, xs, npt = _x_specs(x, tm, tn, sq, lambda j: j)
    return pl.pallas_call(
        functools.partial(_mm_res_kernel, coef=coef, npt=npt),
        grid=(M // tm, N // tn),
        in_specs=[pl.BlockSpec((tm, K), lambda i, j: (i, 0)),
                  _w_spec(w, lead, K, tn, lambda j: j)] + x_specs +
                 [pl.BlockSpec((None, ng, tn), lambda i, j: (layer, i, gcol + j))],
        out_specs=pl.BlockSpec((tm, tn), lambda i, j: (i, j)),
        out_shape=jax.ShapeDtypeStruct((M, N), F32),
        compiler_params=_params(("parallel", "parallel")),
    )(a, w, *xs, modg)


def _gate_proj_kernel(a_ref, w_ref, p1_ref, p2_ref, o_ref, *, kind, H):
    y = jnp.dot(a_ref[...], w_ref[...], preferred_element_type=F32)
    col = lax.broadcasted_iota(jnp.int32, y.shape, 1)
    if kind == "gdn_gla":
        loga = -jnp.exp(p1_ref[...]) * _softplus(y + p2_ref[...])
        o_ref[...] = jnp.where(col < H, loga, jnp.where(col < 2 * H, _sigmoid(y), y))
    else:
        z = y + p1_ref[...]
        o_ref[...] = jnp.where(col < H, z, _log_sigmoid(z))


def _gate_proj(a, w, p1, p2, kind, H):
    M, K = a.shape
    N = w.shape[1]
    tm = _pick(M, 1024, CHUNK)
    return pl.pallas_call(
        functools.partial(_gate_proj_kernel, kind=kind, H=H),
        grid=(M // tm,),
        in_specs=[pl.BlockSpec((tm, K), lambda i: (i, 0)),
                  pl.BlockSpec((K, N), lambda i: (0, 0)),
                  pl.BlockSpec((1, N), lambda i: (0, 0)),
                  pl.BlockSpec((1, N), lambda i: (0, 0))],
        out_specs=pl.BlockSpec((tm, N), lambda i: (i, 0)),
        out_shape=jax.ShapeDtypeStruct((M, N), F32),
        compiler_params=_params(("parallel",)),
    )(a, w, p1, p2)


def _gdn_prep_kernel(q_ref, k_ref, v_ref, pq_ref, pk_ref, pv_ref, csq_ref, csk_ref, csv_ref,
                     cwq_ref, cwk_ref, cwv_ref, gcol_ref, grow_ref,
                     wq_ref, u_ref, pk_out_ref, el_ref, nbq_ref, nbk_ref, nbv_ref,
                     cq_scr, ck_scr, cv_scr, *, HB, dk, dv, sq):
    L = CHUNK
    g = pl.program_id(1)
    _, first, last, in_p = sq.pos(g)
    Wc = cwq_ref.shape[0]
    npv = Wc - 1

    def conv(scr, x_ref, p_ref, cs_ref, cw_ref):
        prev = jnp.where(first, jnp.where(in_p, 0.0, cs_ref[0]), p_ref[8 - npv:8, :])
        scr[8 - npv:8, :] = prev
        scr[8:8 + L, :] = x_ref[...]
        w = cw_ref[...]
        y = scr[8 - npv:8 - npv + L, :] * w[0:1, :]
        for j in range(1, Wc):
            y = y + scr[8 - npv + j:8 - npv + j + L, :] * w[j:j + 1, :]
        return y * _sigmoid(y)

    qc = conv(cq_scr, q_ref, pq_ref, csq_ref, cwq_ref)
    kc = conv(ck_scr, k_ref, pk_ref, csk_ref, cwk_ref)
    vc = conv(cv_scr, v_ref, pv_ref, csv_ref, cwv_ref)

    ii, jj = _tri_masks(L)
    incl = jj <= ii
    strict = jj < ii
    eye = (ii == jj).astype(F32)
    eye_b = eye.astype(BF16)
    gcol = gcol_ref[0]
    gc_all = _dot_hi(incl.astype(F32), gcol[:, :HB])
    gr_all = _dot_hi(grow_ref[0, 0], (ii <= jj).astype(F32))
    n_sq = int(np.log2(L)) - 1

    heads = range(HB)
    sks = [slice(j * dk, (j + 1) * dk) for j in heads]
    svs = [slice(j * dv, (j + 1) * dv) for j in heads]
    kn, gc, beta, eg, gL, kb, Ps, T = ([None] * HB for _ in range(8))
    for j in heads:
        q, k = qc[:, sks[j]], kc[:, sks[j]]
        qn = (q * lax.rsqrt(jnp.sum(q * q, axis=-1, keepdims=True) + EPS)) * (dk ** -0.5)
        kn[j] = k * lax.rsqrt(jnp.sum(k * k, axis=-1, keepdims=True) + EPS)
        gc[j] = gc_all[:, j:j + 1]
        beta[j] = gcol[:, HB + j:HB + j + 1]
        dec_i = jnp.exp(jnp.where(incl, gc[j] - gr_all[j:j + 1, :], -jnp.inf))
        eg[j] = jnp.exp(gc[j])
        gL[j] = gc[j][L - 1:L, :]
        kb[j] = kn[j].astype(BF16)
        qb = qn.astype(BF16)
        wq_ref[0, L:2 * L, sks[j]] = (eg[j] * qn).astype(BF16)
        pk_out_ref[0, j, 0:L, :] = (dec_i * _dot_nt(qb, kb[j])).astype(BF16)
        A = beta[j] * jnp.where(strict, dec_i, 0.0) * _dot_nt(kb[j], kb[j])
        T[j] = eye - A
        Ps[j] = _split_bf16(-A)
    for _ in range(n_sq):
        for j in heads:
            Ps[j] = _split_bf16(_dot_x3(Ps[j], Ps[j]))
        for j in heads:
            T[j] = T[j] + _dot_x3(_split_bf16(T[j]), Ps[j])
    for j in heads:
        rhs = jnp.concatenate([(beta[j] * eg[j]) * kn[j], beta[j] * vc[:, svs[j]]], axis=1)
        WU = _dot(T[j], rhs)
        wq_ref[0, 0:L, sks[j]] = WU[:, :dk].astype(BF16)
        u_ref[:, svs[j]] = WU[:, dk:]
        kw = (kn[j] * jnp.exp(gL[j] - gc[j])).astype(BF16)
        pk_out_ref[0, j, L:L + dk, :] = _dot_tn(kw, eye_b).astype(BF16)
        el_ref[0, :, svs[j]] = jnp.broadcast_to(jnp.exp(gL[j]), (SUBLANE, dv))

    @pl.when(last)
    def _():
        nbq_ref[0] = q_ref[L - npv:L, :]
        nbk_ref[0] = k_ref[L - npv:L, :]
        nbv_ref[0] = v_ref[L - npv:L, :]


def _gdn_prep(proj, col_off, gates_col, gates_row, conv_state, conv_w, H, dk, dv, HB, sq):
    assert dk == dv
    HG = H // HB
    W = HB * dk
    Bt = sq.Bp + sq.Bs
    npv = conv_w.shape[0] - 1
    oq, ok, ov = col_off
    HK = H * dk

    def blk(off):
        return pl.BlockSpec((CHUNK, W), lambda h, g: (g, off // W + h))

    def prev(off):
        return pl.BlockSpec((SUBLANE, W), lambda h, g: (jnp.maximum(g * (CHUNK // SUBLANE) - 1, 0), off // W + h))

    def cst(off):
        return pl.BlockSpec((1, npv, W), lambda h, g: (sq.seq_s(g), 0, off // W + h))

    def cw(off):
        return pl.BlockSpec((conv_w.shape[0], W), lambda h, g: (0, off // W + h))

    def nb():
        return pl.BlockSpec((1, npv, W), lambda h, g: (sq.pos(g)[0], 0, h))

    in_specs = [blk(oq), blk(ok), blk(ov), prev(oq), prev(ok), prev(ov),
                cst(0), cst(HK), cst(2 * HK), cw(0), cw(HK), cw(2 * HK),
                pl.BlockSpec((1, CHUNK, 2 * HB), lambda h, g: (h, g, 0)),
                pl.BlockSpec((1, 1, HB, CHUNK), lambda h, g: (g, h, 0, 0))]
    out_specs = [pl.BlockSpec((1, 2 * CHUNK, W), lambda h, g: (g, 0, h)),
                 pl.BlockSpec((CHUNK, W), lambda h, g: (g, h)),
                 pl.BlockSpec((1, HB, CHUNK + dk, CHUNK), lambda h, g: (g, h, 0, 0)),
                 pl.BlockSpec((1, SUBLANE, W), lambda h, g: (g, 0, h)),
                 nb(), nb(), nb()]
    out_shape = [jax.ShapeDtypeStruct((sq.NG, 2 * CHUNK, HK), BF16),
                 jax.ShapeDtypeStruct((sq.M, HK), F32),
                 jax.ShapeDtypeStruct((sq.NG, H, CHUNK + dk, CHUNK), BF16),
                 jax.ShapeDtypeStruct((sq.NG, SUBLANE, HK), F32),
                 jax.ShapeDtypeStruct((Bt, npv, HK), F32),
                 jax.ShapeDtypeStruct((Bt, npv, HK), F32),
                 jax.ShapeDtypeStruct((Bt, npv, HK), F32)]
    return pl.pallas_call(
        functools.partial(_gdn_prep_kernel, HB=HB, dk=dk, dv=dv, sq=sq),
        grid=(HG, sq.NG),
        in_specs=in_specs,
        out_specs=out_specs,
        out_shape=out_shape,
        scratch_shapes=[pltpu.VMEM((SUBLANE + CHUNK, W), F32)] * 3,
        compiler_params=_params(("parallel", "arbitrary")),
    )(proj, proj, proj, proj, proj, proj, conv_state, conv_state, conv_state,
      conv_w, conv_w, conv_w, gates_col, gates_row)


def _gdn_scan_kernel(wq_ref, u_ref, pk_ref, el_ref, ga_ref, s0_ref, ng_ref, buf_ref,
                     o_ref, sp_ref, ss_ref, s_scr, *, H, dk, dv, sq):
    del buf_ref
    L = CHUNK
    g = pl.program_id(0)
    _, first, last, in_p = sq.pos(g)

    @pl.when(first & in_p)
    def _():
        s_scr[...] = jnp.zeros_like(s_scr)

    @pl.when(first & jnp.logical_not(in_p))
    def _():
        s_scr[...] = s0_ref[0]

    heads = range(H)
    sks = [slice(j * dk, (j + 1) * dk) for j in heads]
    svs = [slice(j * dv, (j + 1) * dv) for j in heads]
    M1, M2 = [None] * H, [None] * H
    for j in heads:
        M1[j] = jnp.dot(wq_ref[0, :, sks[j]], s_scr[j].astype(BF16), preferred_element_type=F32)
    for j in heads:
        E = u_ref[:, svs[j]] - M1[j][:L]
        M2[j] = jnp.dot(pk_ref[0, j], E.astype(BF16), preferred_element_type=F32)
    for j in heads:
        s_scr[j] = el_ref[0, 0:1, svs[j]] * s_scr[j] + M2[j][L:]
        o = M1[j][L:] + M2[j][:L]
        on = (o * lax.rsqrt(jnp.mean(o * o, axis=-1, keepdims=True) + EPS)) * ng_ref[...]
        ga = ga_ref[:, svs[j]]
        o_ref[:, svs[j]] = (on * (ga * _sigmoid(ga))).astype(o_ref.dtype)

    @pl.when(last & in_p)
    def _():
        sp_ref[0] = s_scr[...]

    @pl.when(last & jnp.logical_not(in_p))
    def _():
        ss_ref[0] = s_scr[...]


def _gdn_scan(wq, u, pk, el, proj, gate_off, S0, norm_g, buf, H, dk, dv, sq):
    HK, HV = H * dk, H * dv
    in_specs = [pl.BlockSpec((1, 2 * CHUNK, HK), lambda g: (g, 0, 0)),
                pl.BlockSpec((CHUNK, HV), lambda g: (g, 0)),
                pl.BlockSpec((1, H, CHUNK + dk, CHUNK), lambda g: (g, 0, 0, 0)),
                pl.BlockSpec((1, SUBLANE, HV), lambda g: (g, 0, 0)),
                pl.BlockSpec((CHUNK, HV), lambda g: (g, gate_off // HV)),
                pl.BlockSpec((1, H, dk, dv), lambda g: (sq.seq_s(g), 0, 0, 0)),
                pl.BlockSpec((1, dv), lambda g: (0, 0)),
                pl.BlockSpec(memory_space=pl.ANY)]
    out_specs = [pl.BlockSpec((CHUNK, HV), lambda g: (g, 0)),
                 pl.BlockSpec((1, H, dk, dv), lambda g: (sq.seq_p(g), 0, 0, 0)),
                 pl.BlockSpec((1, H, dk, dv), lambda g: (sq.seq_s(g), 0, 0, 0))]
    out_shape = [jax.ShapeDtypeStruct(buf.shape, buf.dtype),
                 jax.ShapeDtypeStruct((sq.Bp, H, dk, dv), F32),
                 jax.ShapeDtypeStruct((sq.Bs, H, dk, dv), F32)]
    return pl.pallas_call(
        functools.partial(_gdn_scan_kernel, H=H, dk=dk, dv=dv, sq=sq),
        grid=(sq.NG,),
        in_specs=in_specs,
        out_specs=out_specs,
        out_shape=out_shape,
        scratch_shapes=[pltpu.VMEM((H, dk, dv), F32)],
        input_output_aliases={7: 0},
        compiler_params=_params(("arbitrary",)),
    )(wq, u, pk, el, proj, S0, norm_g.reshape(1, dv), buf)


def _gla_kernel(q_ref, k_ref, v_ref, r_ref, lr_ref, w2_ref, gb_ref, s0_ref, ng_ref, buf_ref,
                o_ref, sp_ref, ss_ref, s_scr, *, HB, dk, dv, sq):
    del buf_ref
    L = CHUNK
    g = pl.program_id(1)
    _, first, last, in_p = sq.pos(g)

    @pl.when(first & in_p)
    def _():
        s_scr[...] = jnp.zeros_like(s_scr)

    @pl.when(first & jnp.logical_not(in_p))
    def _():
        s_scr[...] = s0_ref[0]

    ii, jj = _tri_masks(L)
    incl = jj <= ii
    tri = incl.astype(F32)
    krow = lax.broadcasted_iota(jnp.int32, (L, dk), 0)
    di = lax.broadcasted_iota(jnp.int32, (dk, dk), 0)
    dj = lax.broadcasted_iota(jnp.int32, (dk, dk), 1)
    lr = lr_ref[...]

    heads = range(HB)
    sks = [slice(j * dk, (j + 1) * dk) for j in heads]
    svs = [slice(j * dv, (j + 1) * dv) for j in heads]
    G, att = [None] * HB, [None] * HB
    for j in heads:
        x = _dot(lr, w2_ref[:, sks[j]]) + gb_ref[:, sks[j]]
        G[j] = _dot_hi(tri, _log_sigmoid(x) / GLA_TAU)
    for j in heads:
        q = q_ref[:, sks[j]] * (dk ** -0.5)
        k = k_ref[:, sks[j]]
        rows = []
        for b in range(L // GLA_SUB):
            lo, hi = b * GLA_SUB, (b + 1) * GLA_SUB
            ref_g = G[j][lo:lo + 1, :]
            qg = q[lo:hi, :] * jnp.exp(G[j][lo:hi, :] - ref_g)
            kg = k * jnp.exp(jnp.where(krow < hi, ref_g - G[j], -jnp.inf))
            rows.append(_dot_nt(qg, kg))
        att[j] = jnp.where(incl, jnp.concatenate(rows, axis=0), 0.0)
    for j in heads:
        q = q_ref[:, sks[j]] * (dk ** -0.5)
        o = _dot(q * jnp.exp(G[j]), s_scr[j]) + _dot(att[j], v_ref[:, svs[j]])
        on = (o * lax.rsqrt(jnp.mean(o * o, axis=-1, keepdims=True) + EPS)) * ng_ref[...]
        r = r_ref[:, svs[j]]
        o_ref[:, svs[j]] = (on * (r * _sigmoid(r))).astype(o_ref.dtype)
    for j in heads:
        GL = G[j][L - 1:L, :]
        kS = k_ref[:, sks[j]] * jnp.exp(GL - G[j])
        a_col = jnp.sum(jnp.where(di == dj, jnp.broadcast_to(jnp.exp(GL), (dk, dk)), 0.0),
                        axis=1, keepdims=True)
        s_scr[j] = a_col * s_scr[j] + _dot_tn(kS, v_ref[:, svs[j]])

    @pl.when(last & in_p)
    def _():
        sp_ref[0] = s_scr[...]

    @pl.when(last & jnp.logical_not(in_p))
    def _():
        ss_ref[0] = s_scr[...]


def _gla(proj, col_off, lr, w2, gb, S0, norm_g, buf, buf_col_off, H, dk, dv, HB, sq):
    HG = H // HB
    WK, WV = HB * dk, HB * dv
    R = lr.shape[1]
    oq, ok, ov, orr = col_off
    assert oq % WK == 0 and ok % WK == 0 and ov % WV == 0 and orr % WV == 0 and buf_col_off % WV == 0

    def blk(off, w):
        return pl.BlockSpec((CHUNK, w), lambda h, g: (g, off // w + h))

    in_specs = [blk(oq, WK), blk(ok, WK), blk(ov, WV), blk(orr, WV),
                pl.BlockSpec((CHUNK, R), lambda h, g: (g, 0)),
                pl.BlockSpec((R, WK), lambda h, g: (0, h)),
                pl.BlockSpec((1, WK), lambda h, g: (0, h)),
                pl.BlockSpec((1, HB, dk, dv), lambda h, g: (sq.seq_s(g), h, 0, 0)),
                pl.BlockSpec((1, dv), lambda h, g: (0, 0)),
                pl.BlockSpec(memory_space=pl.ANY)]
    out_specs = [pl.BlockSpec((CHUNK, WV), lambda h, g: (g, buf_col_off // WV + h)),
                 pl.BlockSpec((1, HB, dk, dv), lambda h, g: (sq.seq_p(g), h, 0, 0)),
                 pl.BlockSpec((1, HB, dk, dv), lambda h, g: (sq.seq_s(g), h, 0, 0))]
    out_shape = [jax.ShapeDtypeStruct(buf.shape, buf.dtype),
                 jax.ShapeDtypeStruct((sq.Bp, H, dk, dv), F32),
                 jax.ShapeDtypeStruct((sq.Bs, H, dk, dv), F32)]
    return pl.pallas_call(
        functools.partial(_gla_kernel, HB=HB, dk=dk, dv=dv, sq=sq),
        grid=(HG, sq.NG),
        in_specs=in_specs,
        out_specs=out_specs,
        out_shape=out_shape,
        scratch_shapes=[pltpu.VMEM((HB, dk, dv), F32)],
        input_output_aliases={9: 0},
        compiler_params=_params(("parallel", "arbitrary")),
    )(proj, proj, proj, proj, lr, w2, gb.reshape(1, -1), S0, norm_g.reshape(1, dv), buf)


def _mlstm_kernel(q_ref, k_ref, v_ref, og_ref, gcol_ref, grow_ref, c0_ref, n0_ref, m0_ref, ng_ref,
                  o_ref, cp_ref, cs_ref, np_ref, ns_ref, mp_ref, ms_ref, c_scr, n_scr, m_scr,
                  *, HB, dk, dv, sq):
    L = CHUNK
    g = pl.program_id(1)
    _, first, last, in_p = sq.pos(g)

    @pl.when(first & in_p)
    def _():
        c_scr[...] = jnp.zeros_like(c_scr)
        n_scr[...] = jnp.zeros_like(n_scr)
        m_scr[...] = jnp.zeros_like(m_scr)

    @pl.when(first & jnp.logical_not(in_p))
    def _():
        c_scr[...] = c0_ref[0]
        n_scr[...] = n0_ref[0, 0]
        m_scr[...] = m0_ref[0, 0]

    ii, jj = _tri_masks(L)
    incl = jj <= ii
    gcol = gcol_ref[0]
    grow = grow_ref[0, 0]
    bc_all = _dot_hi(incl.astype(F32), gcol[:, HB:])
    br_all = _dot_hi(grow[HB:, :], (ii <= jj).astype(F32))

    heads = range(HB)
    sks = [slice(j * dk, (j + 1) * dk) for j in heads]
    svs = [slice(j * dv, (j + 1) * dv) for j in heads]
    mi, w_inter, expD, wk, decay, qb, Wm = ([None] * HB for _ in range(7))
    for j in heads:
        bc = bc_all[:, j:j + 1]
        m_prev = m_scr[j:j + 1, :]
        Dlog = jnp.where(incl, bc - br_all[j:j + 1, :] + grow[j:j + 1, :], -jnp.inf)
        inter = bc + m_prev
        mi[j] = jnp.maximum(inter, jnp.max(Dlog, axis=-1, keepdims=True))
        w_inter[j] = jnp.exp(inter - mi[j])
        expD[j] = jnp.exp(Dlog - mi[j])
        mL = mi[j][L - 1:L, :]
        bL = bc[L - 1:L, :]
        wk[j] = jnp.exp(bL - bc + gcol[:, j:j + 1] - mL)
        decay[j] = jnp.exp(bL + m_prev - mL)
        m_scr[j:j + 1, :] = mL
    for j in heads:
        qb[j] = (q_ref[:, sks[j]] * (dk ** -0.5)).astype(BF16)
        Wm[j] = expD[j] * _dot_nt(qb[j], k_ref[:, sks[j]])
    for j in heads:
        q = q_ref[:, sks[j]] * (dk ** -0.5)
        num = w_inter[j] * _dot(qb[j], c_scr[j]) + _dot(Wm[j], v_ref[:, svs[j]])
        den = (w_inter[j] * jnp.sum(q * n_scr[j:j + 1, :], axis=-1, keepdims=True)
               + jnp.sum(Wm[j], axis=-1, keepdims=True))
        h = num / jnp.maximum(jnp.abs(den), jnp.exp(-mi[j]))
        hn = (h * lax.rsqrt(jnp.mean(h * h, axis=-1, keepdims=True) + EPS)) * ng_ref[...]
        o_ref[:, svs[j]] = (hn * _sigmoid(og_ref[:, svs[j]])).astype(o_ref.dtype)
    for j in heads:
        kw = k_ref[:, sks[j]] * wk[j]
        c_scr[j] = decay[j] * c_scr[j] + _dot_tn(kw, v_ref[:, svs[j]])
        n_scr[j:j + 1, :] = decay[j] * n_scr[j:j + 1, :] + jnp.sum(kw, axis=0, keepdims=True)

    @pl.when(last & in_p)
    def _():
        cp_ref[0] = c_scr[...]
        np_ref[0, 0] = n_scr[...]
        mp_ref[0, 0] = m_scr[...]

    @pl.when(last & jnp.logical_not(in_p))
    def _():
        cs_ref[0] = c_scr[...]
        ns_ref[0, 0] = n_scr[...]
        ms_ref[0, 0] = m_scr[...]


def _mlstm(proj, col_off, gates_col, gates_row, C0, n0, m0, norm_g, H, dk, dv, HB, sq):
    HG = H // HB
    WK, WV = HB * dk, HB * dv
    oq, ok, ov, oo = col_off

    def blk(off, w):
        return pl.BlockSpec((CHUNK, w), lambda h, g: (g, off // w + h))

    def st(shape, seq_of):
        return pl.BlockSpec((1,) + shape, lambda h, g: (seq_of(g), h) + (0,) * (len(shape) - 1))

    in_specs = [blk(oq, WK), blk(ok, WK), blk(ov, WV), blk(oo, WV),
                pl.BlockSpec((1, CHUNK, 2 * HB), lambda h, g: (h, g, 0)),
                pl.BlockSpec((1, 1, 2 * HB, CHUNK), lambda h, g: (g, h, 0, 0)),
                st((HB, dk, dv), sq.seq_s), st((1, HB, dk), sq.seq_s), st((1, HB, 1), sq.seq_s),
                pl.BlockSpec((1, dv), lambda h, g: (0, 0))]
    out_specs = [pl.BlockSpec((CHUNK, WV), lambda h, g: (g, h)),
                 st((HB, dk, dv), sq.seq_p), st((HB, dk, dv), sq.seq_s),
                 st((1, HB, dk), sq.seq_p), st((1, HB, dk), sq.seq_s),
                 st((1, HB, 1), sq.seq_p), st((1, HB, 1), sq.seq_s)]
    out_shape = [jax.ShapeDtypeStruct((sq.M, H * dv), BF16),
                 jax.ShapeDtypeStruct((sq.Bp, H, dk, dv), F32),
                 jax.ShapeDtypeStruct((sq.Bs, H, dk, dv), F32),
                 jax.ShapeDtypeStruct((sq.Bp, HG, HB, dk), F32),
                 jax.ShapeDtypeStruct((sq.Bs, HG, HB, dk), F32),
                 jax.ShapeDtypeStruct((sq.Bp, HG, HB, 1), F32),
                 jax.ShapeDtypeStruct((sq.Bs, HG, HB, 1), F32)]
    return pl.pallas_call(
        functools.partial(_mlstm_kernel, HB=HB, dk=dk, dv=dv, sq=sq),
        grid=(HG, sq.NG),
        in_specs=in_specs,
        out_specs=out_specs,
        out_shape=out_shape,
        scratch_shapes=[pltpu.VMEM((HB, dk, dv), F32),
                        pltpu.VMEM((HB, dk), F32),
                        pltpu.VMEM((HB, 1), F32)],
        compiler_params=_params(("parallel", "arbitrary")),
    )(proj, proj, proj, proj, gates_col, gates_row, C0,
      n0.reshape(sq.Bs, HG, HB, dk), m0.reshape(sq.Bs, HG, HB, 1), norm_g.reshape(1, dv))


def _gates_col(g, n_kinds, HG, HB):
    M = g.shape[0]
    return g.reshape(M, n_kinds, HG, HB).transpose(2, 0, 1, 3).reshape(HG, M, n_kinds * HB)


def _gates_row(g, n_kinds, HG, HB):
    M = g.shape[0]
    g = g.reshape(M // CHUNK, CHUNK, n_kinds, HG, HB).transpose(0, 3, 2, 4, 1)
    return g.reshape(M // CHUNK, HG, n_kinds * HB, CHUNK)


def kernel(x_prompt, x_sample, state_gdn_S, state_gdn_conv, state_gla_S, state_mlstm_C, state_mlstm_n,
           state_mlstm_m, c_prompt, c_sample, ada_w, ada_b, norm_g, ffn_w_in, ffn_w_out, gdn_gla_w_in,
           gdn_gla_w_out, gdn_conv_w, gdn_A_log, gdn_dt_bias, gdn_norm_g, gla_gate_w2, gla_gate_b,
           gla_norm_g, mlstm_w_in, mlstm_w_out, mlstm_gate_b, mlstm_norm_g, final_norm_g):
    Bp, Tp, D = x_prompt.shape
    Bs, Ts, _ = x_sample.shape
    depth = ada_w.shape[0]
    assert Tp % CHUNK == 0 and Ts % CHUNK == 0 and norm_g.shape[1] == 3
    sq = _Seqs(Bp, Tp // CHUNK, Bs, Ts // CHUNK)
    Mp, Ms = sq.Mp, sq.Ms

    _, _, Hg, dkg, dvg = state_gdn_S.shape
    _, _, Hl, dkl, dvl = state_gla_S.shape
    _, _, Hm, dkm, dvm = state_mlstm_C.shape
    Wg, Wl, Wm = Hg * dvg, Hl * dvl, Hm * dvm
    R = gla_gate_w2.shape[1]
    npv = state_gdn_conv.shape[2]
    assert npv + 1 == gdn_conv_w.shape[1] and npv <= SUBLANE and npv <= CHUNK
    HBg = min(16, Hg)
    HBl = min(4, Hl)
    HBm = min(8, Hm)

    ffn_w_out_b = ffn_w_out.astype(BF16)
    gdn_gla_w_in_b = gdn_gla_w_in.astype(BF16)
    gdn_gla_w_out_b = gdn_gla_w_out.astype(BF16)
    mlstm_w_in_b = mlstm_w_in.astype(BF16)
    mlstm_w_out_b = mlstm_w_out.astype(BF16)

    modg = _ada_mod(jnp.concatenate([c_prompt, c_sample], axis=0), ada_w, ada_b, sq)

    def ffn(x, layer, f, s):
        h = _prenorm(x, norm_g[layer, s], modg, layer, s, sq)
        hh = _matmul_swiglu(h, ffn_w_in, (layer, f))
        return _matmul_residual(hh, ffn_w_out_b, (layer, f), x, modg, layer, s, 0.5, sq, 512)

    x = (x_prompt.reshape(Mp, D), x_sample.reshape(Ms, D))
    new_gS, new_conv, new_lS, new_C, new_n, new_m = [], [], [], [], [], []
    for layer in range(depth):
        x = ffn(x, layer, 0, 0)
        i = layer // 2
        h = _prenorm(x, norm_g[layer, 1], modg, layer, 1, sq)
        if layer % 2 == 0:
            nqk = Hg * dkg
            nA = 2 * nqk + 2 * Wg
            nB = 2 * Hl * dkl + 2 * Wl
            oB = nA + 2 * Hg
            wb = gdn_gla_w_in_b[i]
            w_gate = jnp.concatenate([wb[:, nA:oB], wb[:, oB + nB:]], axis=1)
            ngc = w_gate.shape[1]
            p1 = jnp.zeros((1, ngc), F32).at[0, :Hg].set(gdn_A_log[i].astype(F32))
            p2 = jnp.zeros((1, ngc), F32).at[0, :Hg].set(gdn_dt_bias[i].astype(F32))
            projA = _matmul(h, gdn_gla_w_in_b, (i,), nA, F32)
            projB = _matmul(h, wb[:, oB:oB + nB], (), nB, F32)
            gates = _gate_proj(h, w_gate, p1, p2, "gdn_gla", Hg)
            HGg = Hg // HBg
            gcol = _gates_col(gates[:, :2 * Hg], 2, HGg, HBg)
            grow = _gates_row(gates[:, :Hg], 1, HGg, HBg)
            wq, u, pk, el, nbq, nbk, nbv = _gdn_prep(
                projA, (0, nqk, 2 * nqk), gcol, grow, state_gdn_conv[i].astype(F32),
                gdn_conv_w[i].astype(F32), Hg, dkg, dvg, HBg, sq)
            buf = jnp.zeros((sq.M, Wg + Wl), BF16)
            buf, gS_p, gS_s = _gdn_scan(wq, u, pk, el, projA, 2 * nqk + Wg, state_gdn_S[i].astype(F32),
                                        gdn_norm_g[i].astype(F32), buf, Hg, dkg, dvg, sq)
            oq_b, ok_b, ov_b, or_b = 0, Hl * dkl, 2 * Hl * dkl, 2 * Hl * dkl + Wl
            buf, lS_p, lS_s = _gla(
                projB, (oq_b, ok_b, ov_b, or_b), gates[:, 2 * Hg:], gla_gate_w2[i].astype(F32),
                gla_gate_b[i].astype(F32), state_gla_S[i].astype(F32), gla_norm_g[i].astype(F32),
                buf, Wg, Hl, dkl, dvl, HBl, sq)
            nb = jnp.concatenate([nbq, nbk, nbv], axis=-1)
            new_gS.append((gS_p, gS_s))
            new_conv.append((nb[:Bp], nb[Bp:]))
            new_lS.append((lS_p, lS_s))
            w_out, lead = gdn_gla_w_out_b, (i,)
        else:
            nqk = Hm * dkm
            nmain = 2 * nqk + 2 * Wm
            gb = mlstm_gate_b[i].astype(F32).reshape(1, 2 * Hm)
            proj = _matmul(h, mlstm_w_in_b, (i,), nmain, F32)
            gates = _gate_proj(h, mlstm_w_in_b[i][:, nmain:], gb, gb, "mlstm", Hm)
            HGm = Hm // HBm
            gcol = _gates_col(gates, 2, HGm, HBm)
            grow = _gates_row(gates, 2, HGm, HBm)
            buf, C_p, C_s, n_p, n_s, m_p, m_s = _mlstm(
                proj, (0, nqk, 2 * nqk, 2 * nqk + Wm), gcol, grow, state_mlstm_C[i].astype(F32),
                state_mlstm_n[i].astype(F32), state_mlstm_m[i].astype(F32),
                mlstm_norm_g[i].astype(F32), Hm, dkm, dvm, HBm, sq)
            new_C.append((C_p, C_s))
            new_n.append((n_p.reshape(Bp, Hm, dkm), n_s.reshape(Bs, Hm, dkm)))
            new_m.append((m_p.reshape(Bp, Hm), m_s.reshape(Bs, Hm)))
            w_out, lead = mlstm_w_out_b, (i,)
        x = _matmul_residual(buf, w_out, lead, x, modg, layer, 1, 1.0, sq, 1024)
        x = ffn(x, layer, 1, 2)

    y_p = _final_norm(x, final_norm_g, 0, Mp).reshape(Bp, Tp, D)
    y_s = _final_norm(x, final_norm_g, Mp, Ms).reshape(Bs, Ts, D)

    def pair(states):
        return jnp.stack([p for p, _ in states]), jnp.stack([s for _, s in states])

    gS_p, gS_s = pair(new_gS)
    gc_p, gc_s = pair(new_conv)
    lS_p, lS_s = pair(new_lS)
    C_p, C_s = pair(new_C)
    n_p, n_s = pair(new_n)
    m_p, m_s = pair(new_m)
    return (y_p, y_s, gS_p, gS_s, gc_p, gc_s, lS_p, lS_s, C_p, C_s, n_p, n_s, m_p, m_s)
```

```python
import functools

import numpy as np
import jax
import jax.numpy as jnp
from jax import lax
from jax.experimental import pallas as pl
from jax.experimental.pallas import tpu as pltpu

F32 = jnp.float32
BF16 = jnp.bfloat16
CHUNK = 64
EPS = 1e-6
GLA_TAU = 16.0
GLA_SUB = 16
V7X_VMEM_LIMIT = 58 * 1024 * 1024
LANE = 128
SUBLANE = 8
HI = lax.Precision.HIGHEST


def _pick(dim, target, align):
    best = None
    t = align
    while t <= min(dim, target):
        if dim % t == 0:
            best = t
        t += align
    return dim if best is None else best


def _sigmoid(x):
    return 1.0 / (1.0 + jnp.exp(-x))


def _log_sigmoid(x):
    return jnp.minimum(x, 0.0) - jnp.log1p(jnp.exp(-jnp.abs(x)))


def _softplus(x):
    return jnp.maximum(x, 0.0) + jnp.log1p(jnp.exp(-jnp.abs(x)))


def _dot(a, b):
    return jnp.dot(a.astype(BF16), b.astype(BF16), preferred_element_type=F32)


def _dot_nt(a, b):
    return lax.dot_general(a.astype(BF16), b.astype(BF16), (((1,), (1,)), ((), ())),
                           preferred_element_type=F32)


def _dot_tn(a, b):
    return lax.dot_general(a.astype(BF16), b.astype(BF16), (((0,), (0,)), ((), ())),
                           preferred_element_type=F32)


def _dot_hi(a, b):
    return jnp.dot(a, b, precision=HI, preferred_element_type=F32)


def _split_bf16(a):
    hi = a.astype(BF16)
    return hi, (a - hi.astype(F32)).astype(BF16)


def _dot_x3(a, b):
    (ah, al), (bh, bl) = a, b
    m = ah.shape[0]
    d = functools.partial(jnp.dot, preferred_element_type=F32)
    s = d(jnp.concatenate([ah, al], axis=0), bh)
    return s[:m] + (d(ah, bl) + s[m:])


def _params(sem):
    return pltpu.CompilerParams(dimension_semantics=sem, vmem_limit_bytes=V7X_VMEM_LIMIT)


def _chunk_pos(g, Bp, NCp, NCs):
    npc = Bp * NCp
    in_p = g < npc
    r = g - npc
    seq = jnp.where(in_p, g // NCp, Bp + r // NCs)
    cin = jnp.where(in_p, g % NCp, r % NCs)
    last = jnp.where(in_p, NCp - 1, NCs - 1)
    return seq, cin == 0, cin == last, in_p


def _tri_masks(L):
    ii = lax.broadcasted_iota(jnp.int32, (L, L), 0)
    jj = lax.broadcasted_iota(jnp.int32, (L, L), 1)
    return ii, jj


class _Seqs:
    def __init__(self, Bp, NCp, Bs, NCs):
        self.Bp, self.NCp, self.Bs, self.NCs = Bp, NCp, Bs, NCs
        self.Mp, self.Ms = Bp * NCp * CHUNK, Bs * NCs * CHUNK
        self.M = self.Mp + self.Ms
        self.NG = self.M // CHUNK
        self.Mg = int(np.gcd(self.Mp, self.Ms))

    def pos(self, g):
        return _chunk_pos(g, self.Bp, self.NCp, self.NCs)

    def seq_p(self, g):
        return jnp.minimum(self.pos(g)[0], self.Bp - 1)

    def seq_s(self, g):
        return jnp.maximum(self.pos(g)[0] - self.Bp, 0)


def _x_specs(x, tm, tn, sq, col_of):
    if isinstance(x, tuple):
        npt = sq.Mp // tm
        return [pl.BlockSpec((tm, tn), lambda i, *r: (jnp.minimum(i, npt - 1), col_of(*r))),
                pl.BlockSpec((tm, tn), lambda i, *r: (jnp.maximum(i - npt, 0), col_of(*r)))], list(x), npt
    return [pl.BlockSpec((tm, tn), lambda i, *r: (i, col_of(*r)))], [x], 0


def _on_source(x_refs, npt, body):
    if len(x_refs) == 1:
        body(x_refs[0])
    else:
        i = pl.program_id(0)
        pl.when(i < npt)(lambda: body(x_refs[0]))
        pl.when(i >= npt)(lambda: body(x_refs[1]))


def _ada_kernel(c_ref, w_ref, b_ref, o_ref, res_scr, *, sq):
    c = c_ref[...]
    sc = (c * _sigmoid(c)).astype(BF16)
    res_scr[...] = jnp.dot(sc, w_ref[0].astype(BF16), preferred_element_type=F32) + b_ref[0]
    tn = res_scr.shape[1]
    for b in range(sq.Bp):
        o_ref[0, b * sq.NCp:(b + 1) * sq.NCp, :] = jnp.broadcast_to(res_scr[b:b + 1, :], (sq.NCp, tn))
    base = sq.Bp * sq.NCp
    if sq.NCs == 1:
        o_ref[0, base:base + sq.Bs, :] = res_scr[sq.Bp:sq.Bp + sq.Bs, :]
    else:
        for b in range(sq.Bs):
            o_ref[0, base + b * sq.NCs:base + (b + 1) * sq.NCs, :] = jnp.broadcast_to(
                res_scr[sq.Bp + b:sq.Bp + b + 1, :], (sq.NCs, tn))


def _ada_mod(c, ada_w, ada_b, sq):
    depth, D, N = ada_w.shape
    Bt = c.shape[0]
    tn = _pick(N, 512, LANE)
    return pl.pallas_call(
        functools.partial(_ada_kernel, sq=sq),
        grid=(depth, N // tn),
        in_specs=[pl.BlockSpec((Bt, D), lambda l, j: (0, 0)),
                  pl.BlockSpec((1, D, tn), lambda l, j: (l, 0, j)),
                  pl.BlockSpec((1, 1, tn), lambda l, j: (l, 0, j))],
        out_specs=pl.BlockSpec((1, sq.NG, tn), lambda l, j: (l, 0, j)),
        out_shape=jax.ShapeDtypeStruct((depth, sq.NG, N), F32),
        scratch_shapes=[pltpu.VMEM((Bt, tn), F32)],
        compiler_params=_params(("parallel", "parallel")),
    )(c, ada_w, ada_b.reshape(depth, 1, N))


def _row_rms_scale(x_ref, rows):
    D = x_ref.shape[1]
    acc = None
    for c in range(0, D, LANE):
        xc = x_ref[rows, c:c + LANE]
        acc = xc * xc if acc is None else acc + xc * xc
    return lax.rsqrt(jnp.sum(acc, axis=-1, keepdims=True) * (1.0 / D) + EPS)


def _prenorm_kernel(*refs, npt):
    *x_refs, g_ref, sc_ref, sh_ref, o_ref = refs

    def body(x_ref):
        D = x_ref.shape[1]
        cw = _pick(D, 512, LANE)
        for r in range(x_ref.shape[0] // CHUNK):
            rows = slice(r * CHUNK, (r + 1) * CHUNK)
            rinv = _row_rms_scale(x_ref, rows)
            for c in range(0, D, cw):
                cols = slice(c, c + cw)
                gm = g_ref[:, cols] * (1.0 + sc_ref[r:r + 1, cols])
                o_ref[rows, cols] = ((x_ref[rows, cols] * rinv) * gm + sh_ref[r:r + 1, cols]).astype(o_ref.dtype)

    _on_source(x_refs, npt, body)


def _prenorm(x, g, modg, layer, s, sq):
    D = g.shape[0]
    tr = _pick(sq.Mg, 512, CHUNK * SUBLANE)
    ng = tr // CHUNK
    x_specs, xs, npt = _x_specs(x, tr, D, sq, lambda: 0)
    return pl.pallas_call(
        functools.partial(_prenorm_kernel, npt=npt),
        grid=(sq.M // tr,),
        in_specs=x_specs + [pl.BlockSpec((1, D), lambda i: (0, 0)),
                            pl.BlockSpec((None, ng, D), lambda i: (layer, i, s * 3 + 1)),
                            pl.BlockSpec((None, ng, D), lambda i: (layer, i, s * 3))],
        out_specs=pl.BlockSpec((tr, D), lambda i: (i, 0)),
        out_shape=jax.ShapeDtypeStruct((sq.M, D), BF16),
        compiler_params=_params(("parallel",)),
    )(*xs, g.reshape(1, D), modg, modg)


def _final_norm_kernel(x_ref, g_ref, o_ref):
    D = x_ref.shape[1]
    cw = _pick(D, 512, LANE)
    rt = _pick(x_ref.shape[0], CHUNK, SUBLANE)
    for r in range(0, x_ref.shape[0], rt):
        rows = slice(r, r + rt)
        rinv = _row_rms_scale(x_ref, rows)
        for c in range(0, D, cw):
            cols = slice(c, c + cw)
            o_ref[rows, cols] = (x_ref[rows, cols] * rinv) * g_ref[:, cols]


def _final_norm(x, g, row0, nrows):
    D = x.shape[1]
    tr = _pick(int(np.gcd(nrows, row0)) if row0 else nrows, 256, SUBLANE)
    assert row0 % tr == 0 and nrows % tr == 0
    off = row0 // tr
    return pl.pallas_call(
        _final_norm_kernel,
        grid=(nrows // tr,),
        in_specs=[pl.BlockSpec((tr, D), lambda i: (i + off, 0)),
                  pl.BlockSpec((1, D), lambda i: (0, 0))],
        out_specs=pl.BlockSpec((tr, D), lambda i: (i, 0)),
        out_shape=jax.ShapeDtypeStruct((nrows, D), F32),
        compiler_params=_params(("parallel",)),
    )(x, g.reshape(1, D))


def _w_spec(w, lead, K, tn, col_of):
    return pl.BlockSpec((None,) * len(lead) + (K, tn), lambda i, j: tuple(lead) + (0, col_of(j)))


def _mm_kernel(a_ref, w_ref, o_ref):
    o_ref[...] = jnp.dot(a_ref[...], w_ref[...], preferred_element_type=F32).astype(o_ref.dtype)


def _matmul(a, w, lead, ncols, out_dtype):
    M, K = a.shape
    tm = _pick(M, 1024, CHUNK)
    tn = _pick(ncols, 1024, LANE)
    return pl.pallas_call(
        _mm_kernel,
        grid=(M // tm, ncols // tn),
        in_specs=[pl.BlockSpec((tm, K), lambda i, j: (i, 0)),
                  _w_spec(w, lead, K, tn, lambda j: j)],
        out_specs=pl.BlockSpec((tm, tn), lambda i, j: (i, j)),
        out_shape=jax.ShapeDtypeStruct((M, ncols), out_dtype),
        compiler_params=_params(("parallel", "parallel")),
    )(a, w)


def _mm_swiglu_kernel(a_ref, wg_ref, wu_ref, o_ref):
    a = a_ref[...]
    g = jnp.dot(a, wg_ref[...].astype(BF16), preferred_element_type=F32)
    u = jnp.dot(a, wu_ref[...].astype(BF16), preferred_element_type=F32)
    rt = _pick(g.shape[0], 256, SUBLANE)
    for r in range(0, g.shape[0], rt):
        gr = g[r:r + rt, :]
        o_ref[r:r + rt, :] = ((gr * _sigmoid(gr)) * u[r:r + rt, :]).astype(o_ref.dtype)


def _matmul_swiglu(a, w, lead):
    M, K = a.shape
    F = w.shape[-1] // 2
    tm = _pick(M, 3072, CHUNK)
    tn = _pick(F, 256, LANE)
    nf = F // tn
    return pl.pallas_call(
        _mm_swiglu_kernel,
        grid=(M // tm, nf),
        in_specs=[pl.BlockSpec((tm, K), lambda i, j: (i, 0), pipeline_mode=pl.Buffered(1)),
                  _w_spec(w, lead, K, tn, lambda j: j),
                  _w_spec(w, lead, K, tn, lambda j: nf + j)],
        out_specs=pl.BlockSpec((tm, tn), lambda i, j: (i, j)),
        out_shape=jax.ShapeDtypeStruct((M, F), BF16),
        compiler_params=_params(("parallel", "parallel")),
    )(a, w, w)


def _mm_res_kernel(a_ref, w_ref, *refs, coef, npt):
    *x_refs, gate_ref, o_ref = refs
    y = jnp.dot(a_ref[...], w_ref[...], preferred_element_type=F32)

    def body(x_ref):
        for r in range(y.shape[0] // CHUNK):
            rows = slice(r * CHUNK, (r + 1) * CHUNK)
            o_ref[rows, :] = x_ref[rows, :] + (coef * gate_ref[r:r + 1, :]) * y[rows, :]

    _on_source(x_refs, npt, body)


def _matmul_residual(a, w, lead, x, modg, layer, s, coef, sq, tile):
    M, K = a.shape
    N = w.shape[-1]
    tm = _pick(sq.Mg, tile, CHUNK * SUBLANE)
    tn = _pick(N, tile, LANE)
    ng = tm // CHUNK
    gcol = (s * 3 + 2) * (N // tn)
    x_specs, xs, npt = _x_specs(x, tm, tn, sq, lambda j: j)
    return pl.pallas_call(
        functools.partial(_mm_res_kernel, coef=coef, npt=npt),
        grid=(M // tm, N // tn),
        in_specs=[pl.BlockSpec((tm, K), lambda i, j: (i, 0)),
                  _w_spec(w, lead, K, tn, lambda j: j)] + x_specs +
                 [pl.BlockSpec((None, ng, tn), lambda i, j: (layer, i, gcol + j))],
        out_specs=pl.BlockSpec((tm, tn), lambda i, j: (i, j)),
        out_shape=jax.ShapeDtypeStruct((M, N), F32),
        compiler_params=_params(("parallel", "parallel")),
    )(a, w, *xs, modg)


def _gate_proj_kernel(a_ref, w_ref, p1_ref, p2_ref, o_ref, *, kind, H):
    y = jnp.dot(a_ref[...], w_ref[...], preferred_element_type=F32)
    col = lax.broadcasted_iota(jnp.int32, y.shape, 1)
    if kind == "gdn_gla":
        loga = -jnp.exp(p1_ref[...]) * _softplus(y + p2_ref[...])
        o_ref[...] = jnp.where(col < H, loga, jnp.where(col < 2 * H, _sigmoid(y), y))
    else:
        z = y + p1_ref[...]
        o_ref[...] = jnp.where(col < H, z, _log_sigmoid(z))


def _gate_proj(a, w, p1, p2, kind, H):
    M, K = a.shape
    N = w.shape[1]
    tm = _pick(M, 1024, CHUNK)
    return pl.pallas_call(
        functools.partial(_gate_proj_kernel, kind=kind, H=H),
        grid=(M // tm,),
        in_specs=[pl.BlockSpec((tm, K), lambda i: (i, 0)),
                  pl.BlockSpec((K, N), lambda i: (0, 0)),
                  pl.BlockSpec((1, N), lambda i: (0, 0)),
                  pl.BlockSpec((1, N), lambda i: (0, 0))],
        out_specs=pl.BlockSpec((tm, N), lambda i: (i, 0)),
        out_shape=jax.ShapeDtypeStruct((M, N), F32),
        compiler_params=_params(("parallel",)),
    )(a, w, p1, p2)


def _gdn_prep_kernel(q_ref, k_ref, v_ref, pq_ref, pk_ref, pv_ref, csq_ref, csk_ref, csv_ref,
                     cwq_ref, cwk_ref, cwv_ref, gcol_ref, grow_ref,
                     wq_ref, u_ref, pk_out_ref, el_ref, nbq_ref, nbk_ref, nbv_ref,
                     cq_scr, ck_scr, cv_scr, *, HB, dk, dv, sq):
    L = CHUNK
    g = pl.program_id(1)
    _, first, last, in_p = sq.pos(g)
    Wc = cwq_ref.shape[0]
    npv = Wc - 1

    def conv(scr, x_ref, p_ref, cs_ref, cw_ref):
        prev = jnp.where(first, jnp.where(in_p, 0.0, cs_ref[0]), p_ref[8 - npv:8, :])
        scr[8 - npv:8, :] = prev
        scr[8:8 + L, :] = x_ref[...]
        w = cw_ref[...]
        y = scr[8 - npv:8 - npv + L, :] * w[0:1, :]
        for j in range(1, Wc):
            y = y + scr[8 - npv + j:8 - npv + j + L, :] * w[j:j + 1, :]
        return y * _sigmoid(y)

    qc = conv(cq_scr, q_ref, pq_ref, csq_ref, cwq_ref)
    kc = conv(ck_scr, k_ref, pk_ref, csk_ref, cwk_ref)
    vc = conv(cv_scr, v_ref, pv_ref, csv_ref, cwv_ref)

    ii, jj = _tri_masks(L)
    incl = jj <= ii
    strict = jj < ii
    eye = (ii == jj).astype(F32)
    eye_b = eye.astype(BF16)
    gcol = gcol_ref[0]
    gc_all = _dot_hi(incl.astype(F32), gcol[:, :HB])
    gr_all = _dot_hi(grow_ref[0, 0], (ii <= jj).astype(F32))
    n_sq = int(np.log2(L)) - 1

    heads = range(HB)
    sks = [slice(j * dk, (j + 1) * dk) for j in heads]
    svs = [slice(j * dv, (j + 1) * dv) for j in heads]
    kn, gc, beta, eg, gL, kb, Ps, T = ([None] * HB for _ in range(8))
    for j in heads:
        q, k = qc[:, sks[j]], kc[:, sks[j]]
        qn = (q * lax.rsqrt(jnp.sum(q * q, axis=-1, keepdims=True) + EPS)) * (dk ** -0.5)
        kn[j] = k * lax.rsqrt(jnp.sum(k * k, axis=-1, keepdims=True) + EPS)
        gc[j] = gc_all[:, j:j + 1]
        beta[j] = gcol[:, HB + j:HB + j + 1]
        dec_i = jnp.exp(jnp.where(incl, gc[j] - gr_all[j:j + 1, :], -jnp.inf))
        eg[j] = jnp.exp(gc[j])
        gL[j] = gc[j][L - 1:L, :]
        kb[j] = kn[j].astype(BF16)
        qb = qn.astype(BF16)
        wq_ref[0, L:2 * L, sks[j]] = (eg[j] * qn).astype(BF16)
        pk_out_ref[0, j, 0:L, :] = (dec_i * _dot_nt(qb, kb[j])).astype(BF16)
        A = beta[j] * jnp.where(strict, dec_i, 0.0) * _dot_nt(kb[j], kb[j])
        T[j] = eye - A
        Ps[j] = _split_bf16(-A)
    for _ in range(n_sq):
        for j in heads:
            Ps[j] = _split_bf16(_dot_x3(Ps[j], Ps[j]))
        for j in heads:
            T[j] = T[j] + _dot_x3(_split_bf16(T[j]), Ps[j])
    for j in heads:
        rhs = jnp.concatenate([(beta[j] * eg[j]) * kn[j], beta[j] * vc[:, svs[j]]], axis=1)
        WU = _dot(T[j], rhs)
        wq_ref[0, 0:L, sks[j]] = WU[:, :dk].astype(BF16)
        u_ref[:, svs[j]] = WU[:, dk:]
        kw = (kn[j] * jnp.exp(gL[j] - gc[j])).astype(BF16)
        pk_out_ref[0, j, L:L + dk, :] = _dot_tn(kw, eye_b).astype(BF16)
        el_ref[0, :, svs[j]] = jnp.broadcast_to(jnp.exp(gL[j]), (SUBLANE, dv))

    @pl.when(last)
    def _():
        nbq_ref[0] = q_ref[L - npv:L, :]
        nbk_ref[0] = k_ref[L - npv:L, :]
        nbv_ref[0] = v_ref[L - npv:L, :]


def _gdn_prep(proj, col_off, gates_col, gates_row, conv_state, conv_w, H, dk, dv, HB, sq):
    assert dk == dv
    HG = H // HB
    W = HB * dk
    Bt = sq.Bp + sq.Bs
    npv = conv_w.shape[0] - 1
    oq, ok, ov = col_off
    HK = H * dk

    def blk(off):
        return pl.BlockSpec((CHUNK, W), lambda h, g: (g, off // W + h))

    def prev(off):
        return pl.BlockSpec((SUBLANE, W), lambda h, g: (jnp.maximum(g * (CHUNK // SUBLANE) - 1, 0), off // W + h))

    def cst(off):
        return pl.BlockSpec((1, npv, W), lambda h, g: (sq.seq_s(g), 0, off // W + h))

    def cw(off):
        return pl.BlockSpec((conv_w.shape[0], W), lambda h, g: (0, off // W + h))

    def nb():
        return pl.BlockSpec((1, npv, W), lambda h, g: (sq.pos(g)[0], 0, h))

    in_specs = [blk(oq), blk(ok), blk(ov), prev(oq), prev(ok), prev(ov),
                cst(0), cst(HK), cst(2 * HK), cw(0), cw(HK), cw(2 * HK),
                pl.BlockSpec((1, CHUNK, 2 * HB), lambda h, g: (h, g, 0)),
                pl.BlockSpec((1, 1, HB, CHUNK), lambda h, g: (g, h, 0, 0))]
    out_specs = [pl.BlockSpec((1, 2 * CHUNK, W), lambda h, g: (g, 0, h)),
                 pl.BlockSpec((CHUNK, W), lambda h, g: (g, h)),
                 pl.BlockSpec((1, HB, CHUNK + dk, CHUNK), lambda h, g: (g, h, 0, 0)),
                 pl.BlockSpec((1, SUBLANE, W), lambda h, g: (g, 0, h)),
                 nb(), nb(), nb()]
    out_shape = [jax.ShapeDtypeStruct((sq.NG, 2 * CHUNK, HK), BF16),
                 jax.ShapeDtypeStruct((sq.M, HK), F32),
                 jax.ShapeDtypeStruct((sq.NG, H, CHUNK + dk, CHUNK), BF16),
                 jax.ShapeDtypeStruct((sq.NG, SUBLANE, HK), F32),
                 jax.ShapeDtypeStruct((Bt, npv, HK), F32),
                 jax.ShapeDtypeStruct((Bt, npv, HK), F32),
                 jax.ShapeDtypeStruct((Bt, npv, HK), F32)]
    return pl.pallas_call(
        functools.partial(_gdn_prep_kernel, HB=HB, dk=dk, dv=dv, sq=sq),
        grid=(HG, sq.NG),
        in_specs=in_specs,
        out_specs=out_specs,
        out_shape=out_shape,
        scratch_shapes=[pltpu.VMEM((SUBLANE + CHUNK, W), F32)] * 3,
        compiler_params=_params(("parallel", "arbitrary")),
    )(proj, proj, proj, proj, proj, proj, conv_state, conv_state, conv_state,
      conv_w, conv_w, conv_w, gates_col, gates_row)


def _gdn_scan_kernel(wq_ref, u_ref, pk_ref, el_ref, ga_ref, s0_ref, ng_ref, buf_ref,
                     o_ref, sp_ref, ss_ref, s_scr, *, H, dk, dv, sq):
    del buf_ref
    L = CHUNK
    g = pl.program_id(0)
    _, first, last, in_p = sq.pos(g)

    @pl.when(first & in_p)
    def _():
        s_scr[...] = jnp.zeros_like(s_scr)

    @pl.when(first & jnp.logical_not(in_p))
    def _():
        s_scr[...] = s0_ref[0]

    heads = range(H)
    sks = [slice(j * dk, (j + 1) * dk) for j in heads]
    svs = [slice(j * dv, (j + 1) * dv) for j in heads]
    M1, M2 = [None] * H, [None] * H
    for j in heads:
        M1[j] = jnp.dot(wq_ref[0, :, sks[j]], s_scr[j].astype(BF16), preferred_element_type=F32)
    for j in heads:
        E = u_ref[:, svs[j]] - M1[j][:L]
        M2[j] = jnp.dot(pk_ref[0, j], E.astype(BF16), preferred_element_type=F32)
    for j in heads:
        s_scr[j] = el_ref[0, 0:1, svs[j]] * s_scr[j] + M2[j][L:]
        o = M1[j][L:] + M2[j][:L]
        on = (o * lax.rsqrt(jnp.mean(o * o, axis=-1, keepdims=True) + EPS)) * ng_ref[...]
        ga = ga_ref[:, svs[j]]
        o_ref[:, svs[j]] = (on * (ga * _sigmoid(ga))).astype(o_ref.dtype)

    @pl.when(last & in_p)
    def _():
        sp_ref[0] = s_scr[...]

    @pl.when(last & jnp.logical_not(in_p))
    def _():
        ss_ref[0] = s_scr[...]


def _gdn_scan(wq, u, pk, el, proj, gate_off, S0, norm_g, buf, H, dk, dv, sq):
    HK, HV = H * dk, H * dv
    in_specs = [pl.BlockSpec((1, 2 * CHUNK, HK), lambda g: (g, 0, 0)),
                pl.BlockSpec((CHUNK, HV), lambda g: (g, 0)),
                pl.BlockSpec((1, H, CHUNK + dk, CHUNK), lambda g: (g, 0, 0, 0)),
                pl.BlockSpec((1, SUBLANE, HV), lambda g: (g, 0, 0)),
                pl.BlockSpec((CHUNK, HV), lambda g: (g, gate_off // HV)),
                pl.BlockSpec((1, H, dk, dv), lambda g: (sq.seq_s(g), 0, 0, 0)),
                pl.BlockSpec((1, dv), lambda g: (0, 0)),
                pl.BlockSpec(memory_space=pl.ANY)]
    out_specs = [pl.BlockSpec((CHUNK, HV), lambda g: (g, 0)),
                 pl.BlockSpec((1, H, dk, dv), lambda g: (sq.seq_p(g), 0, 0, 0)),
                 pl.BlockSpec((1, H, dk, dv), lambda g: (sq.seq_s(g), 0, 0, 0))]
    out_shape = [jax.ShapeDtypeStruct(buf.shape, buf.dtype),
                 jax.ShapeDtypeStruct((sq.Bp, H, dk, dv), F32),
                 jax.ShapeDtypeStruct((sq.Bs, H, dk, dv), F32)]
    return pl.pallas_call(
        functools.partial(_gdn_scan_kernel, H=H, dk=dk, dv=dv, sq=sq),
        grid=(sq.NG,),
        in_specs=in_specs,
        out_specs=out_specs,
        out_shape=out_shape,
        scratch_shapes=[pltpu.VMEM((H, dk, dv), F32)],
        input_output_aliases={7: 0},
        compiler_params=_params(("arbitrary",)),
    )(wq, u, pk, el, proj, S0, norm_g.reshape(1, dv), buf)


def _gla_kernel(q_ref, k_ref, v_ref, r_ref, lr_ref, w2_ref, gb_ref, s0_ref, ng_ref, buf_ref,
                o_ref, sp_ref, ss_ref, s_scr, *, HB, dk, dv, sq):
    del buf_ref
    L = CHUNK
    g = pl.program_id(1)
    _, first, last, in_p = sq.pos(g)

    @pl.when(first & in_p)
    def _():
        s_scr[...] = jnp.zeros_like(s_scr)

    @pl.when(first & jnp.logical_not(in_p))
    def _():
        s_scr[...] = s0_ref[0]

    ii, jj = _tri_masks(L)
    incl = jj <= ii
    tri = incl.astype(F32)
    krow = lax.broadcasted_iota(jnp.int32, (L, dk), 0)
    di = lax.broadcasted_iota(jnp.int32, (dk, dk), 0)
    dj = lax.broadcasted_iota(jnp.int32, (dk, dk), 1)
    lr = lr_ref[...]

    heads = range(HB)
    sks = [slice(j * dk, (j + 1) * dk) for j in heads]
    svs = [slice(j * dv, (j + 1) * dv) for j in heads]
    G, att = [None] * HB, [None] * HB
    for j in heads:
        x = _dot(lr, w2_ref[:, sks[j]]) + gb_ref[:, sks[j]]
        G[j] = _dot_hi(tri, _log_sigmoid(x) / GLA_TAU)
    for j in heads:
        q = q_ref[:, sks[j]] * (dk ** -0.5)
        k = k_ref[:, sks[j]]
        rows = []
        for b in range(L // GLA_SUB):
            lo, hi = b * GLA_SUB, (b + 1) * GLA_SUB
            mid = lo + GLA_SUB // 2
            ref_g = G[j][mid:mid + 1, :]
            qg = q[lo:hi, :] * jnp.exp(G[j][lo:hi, :] - ref_g)
            kg = k * jnp.exp(jnp.where(krow < hi, ref_g - G[j], -jnp.inf))
            rows.append(_dot_nt(qg, kg))
        att[j] = jnp.where(incl, jnp.concatenate(rows, axis=0), 0.0)
    for j in heads:
        q = q_ref[:, sks[j]] * (dk ** -0.5)
        o = _dot(q * jnp.exp(G[j]), s_scr[j]) + _dot(att[j], v_ref[:, svs[j]])
        on = (o * lax.rsqrt(jnp.mean(o * o, axis=-1, keepdims=True) + EPS)) * ng_ref[...]
        r = r_ref[:, svs[j]]
        o_ref[:, svs[j]] = (on * (r * _sigmoid(r))).astype(o_ref.dtype)
    for j in heads:
        GL = G[j][L - 1:L, :]
        kS = k_ref[:, sks[j]] * jnp.exp(GL - G[j])
        a_col = jnp.sum(jnp.where(di == dj, jnp.broadcast_to(jnp.exp(GL), (dk, dk)), 0.0),
                        axis=1, keepdims=True)
        s_scr[j] = a_col * s_scr[j] + _dot_tn(kS, v_ref[:, svs[j]])

    @pl.when(last & in_p)
    def _():
        sp_ref[0] = s_scr[...]

    @pl.when(last & jnp.logical_not(in_p))
    def _():
        ss_ref[0] = s_scr[...]


def _gla(proj, col_off, lr, w2, gb, S0, norm_g, buf, buf_col_off, H, dk, dv, HB, sq):
    HG = H // HB
    WK, WV = HB * dk, HB * dv
    R = lr.shape[1]
    oq, ok, ov, orr = col_off
    assert oq % WK == 0 and ok % WK == 0 and ov % WV == 0 and orr % WV == 0 and buf_col_off % WV == 0

    def blk(off, w):
        return pl.BlockSpec((CHUNK, w), lambda h, g: (g, off // w + h))

    in_specs = [blk(oq, WK), blk(ok, WK), blk(ov, WV), blk(orr, WV),
                pl.BlockSpec((CHUNK, R), lambda h, g: (g, 0)),
                pl.BlockSpec((R, WK), lambda h, g: (0, h)),
                pl.BlockSpec((1, WK), lambda h, g: (0, h)),
                pl.BlockSpec((1, HB, dk, dv), lambda h, g: (sq.seq_s(g), h, 0, 0)),
                pl.BlockSpec((1, dv), lambda h, g: (0, 0)),
                pl.BlockSpec(memory_space=pl.ANY)]
    out_specs = [pl.BlockSpec((CHUNK, WV), lambda h, g: (g, buf_col_off // WV + h)),
                 pl.BlockSpec((1, HB, dk, dv), lambda h, g: (sq.seq_p(g), h, 0, 0)),
                 pl.BlockSpec((1, HB, dk, dv), lambda h, g: (sq.seq_s(g), h, 0, 0))]
    out_shape = [jax.ShapeDtypeStruct(buf.shape, buf.dtype),
                 jax.ShapeDtypeStruct((sq.Bp, H, dk, dv), F32),
                 jax.ShapeDtypeStruct((sq.Bs, H, dk, dv), F32)]
    return pl.pallas_call(
        functools.partial(_gla_kernel, HB=HB, dk=dk, dv=dv, sq=sq),
        grid=(HG, sq.NG),
        in_specs=in_specs,
        out_specs=out_specs,
        out_shape=out_shape,
        scratch_shapes=[pltpu.VMEM((HB, dk, dv), F32)],
        input_output_aliases={9: 0},
        compiler_params=_params(("parallel", "arbitrary")),
    )(proj, proj, proj, proj, lr, w2, gb.reshape(1, -1), S0, norm_g.reshape(1, dv), buf)


def _mlstm_kernel(q_ref, k_ref, v_ref, og_ref, gcol_ref, grow_ref, c0_ref, n0_ref, m0_ref, ng_ref,
                  o_ref, cp_ref, cs_ref, np_ref, ns_ref, mp_ref, ms_ref, c_scr, n_scr, m_scr,
                  *, HB, dk, dv, sq):
    L = CHUNK
    g = pl.program_id(1)
    _, first, last, in_p = sq.pos(g)

    @pl.when(first & in_p)
    def _():
        c_scr[...] = jnp.zeros_like(c_scr)
        n_scr[...] = jnp.zeros_like(n_scr)
        m_scr[...] = jnp.zeros_like(m_scr)

    @pl.when(first & jnp.logical_not(in_p))
    def _():
        c_scr[...] = c0_ref[0]
        n_scr[...] = n0_ref[0, 0]
        m_scr[...] = m0_ref[0, 0]

    ii, jj = _tri_masks(L)
    incl = jj <= ii
    gcol = gcol_ref[0]
    grow = grow_ref[0, 0]
    bc_all = _dot_hi(incl.astype(F32), gcol[:, HB:])
    br_all = _dot_hi(grow[HB:, :], (ii <= jj).astype(F32))

    heads = range(HB)
    sks = [slice(j * dk, (j + 1) * dk) for j in heads]
    svs = [slice(j * dv, (j + 1) * dv) for j in heads]
    mi, w_inter, expD, wk, decay, qb, Wm = ([None] * HB for _ in range(7))
    for j in heads:
        bc = bc_all[:, j:j + 1]
        m_prev = m_scr[j:j + 1, :]
        Dlog = jnp.where(incl, bc - br_all[j:j + 1, :] + grow[j:j + 1, :], -jnp.inf)
        inter = bc + m_prev
        mi[j] = jnp.maximum(inter, jnp.max(Dlog, axis=-1, keepdims=True))
        w_inter[j] = jnp.exp(inter - mi[j])
        expD[j] = jnp.exp(Dlog - mi[j])
        mL = mi[j][L - 1:L, :]
        bL = bc[L - 1:L, :]
        wk[j] = jnp.exp(bL - bc + gcol[:, j:j + 1] - mL)
        decay[j] = jnp.exp(bL + m_prev - mL)
        m_scr[j:j + 1, :] = mL
    for j in heads:
        qb[j] = (q_ref[:, sks[j]] * (dk ** -0.5)).astype(BF16)
        Wm[j] = expD[j] * _dot_nt(qb[j], k_ref[:, sks[j]])
    for j in heads:
        q = q_ref[:, sks[j]] * (dk ** -0.5)
        num = w_inter[j] * _dot(qb[j], c_scr[j]) + _dot(Wm[j], v_ref[:, svs[j]])
        den = (w_inter[j] * jnp.sum(q * n_scr[j:j + 1, :], axis=-1, keepdims=True)
               + jnp.sum(Wm[j], axis=-1, keepdims=True))
        h = num / jnp.maximum(jnp.abs(den), jnp.exp(-mi[j]))
        hn = (h * lax.rsqrt(jnp.mean(h * h, axis=-1, keepdims=True) + EPS)) * ng_ref[...]
        o_ref[:, svs[j]] = (hn * _sigmoid(og_ref[:, svs[j]])).astype(o_ref.dtype)
    for j in heads:
        kw = k_ref[:, sks[j]] * wk[j]
        c_scr[j] = decay[j] * c_scr[j] + _dot_tn(kw, v_ref[:, svs[j]])
        n_scr[j:j + 1, :] = decay[j] * n_scr[j:j + 1, :] + jnp.sum(kw, axis=0, keepdims=True)

    @pl.when(last & in_p)
    def _():
        cp_ref[0] = c_scr[...]
        np_ref[0, 0] = n_scr[...]
        mp_ref[0, 0] = m_scr[...]

    @pl.when(last & jnp.logical_not(in_p))
    def _():
        cs_ref[0] = c_scr[...]
        ns_ref[0, 0] = n_scr[...]
        ms_ref[0, 0] = m_scr[...]


def _mlstm(proj, col_off, gates_col, gates_row, C0, n0, m0, norm_g, H, dk, dv, HB, sq):
    HG = H // HB
    WK, WV = HB * dk, HB * dv
    oq, ok, ov, oo = col_off

    def blk(off, w):
        return pl.BlockSpec((CHUNK, w), lambda h, g: (g, off // w + h))

    def st(shape, seq_of):
        return pl.BlockSpec((1,) + shape, lambda h, g: (seq_of(g), h) + (0,) * (len(shape) - 1))

    in_specs = [blk(oq, WK), blk(ok, WK), blk(ov, WV), blk(oo, WV),
                pl.BlockSpec((1, CHUNK, 2 * HB), lambda h, g: (h, g, 0)),
                pl.BlockSpec((1, 1, 2 * HB, CHUNK), lambda h, g: (g, h, 0, 0)),
                st((HB, dk, dv), sq.seq_s), st((1, HB, dk), sq.seq_s), st((1, HB, 1), sq.seq_s),
                pl.BlockSpec((1, dv), lambda h, g: (0, 0))]
    out_specs = [pl.BlockSpec((CHUNK, WV), lambda h, g: (g, h)),
                 st((HB, dk, dv), sq.seq_p), st((HB, dk, dv), sq.seq_s),
                 st((1, HB, dk), sq.seq_p), st((1, HB, dk), sq.seq_s),
                 st((1, HB, 1), sq.seq_p), st((1, HB, 1), sq.seq_s)]
    out_shape = [jax.ShapeDtypeStruct((sq.M, H * dv), BF16),
                 jax.ShapeDtypeStruct((sq.Bp, H, dk, dv), F32),
                 jax.ShapeDtypeStruct((sq.Bs, H, dk, dv), F32),
                 jax.ShapeDtypeStruct((sq.Bp, HG, HB, dk), F32),
                 jax.ShapeDtypeStruct((sq.Bs, HG, HB, dk), F32),
                 jax.ShapeDtypeStruct((sq.Bp, HG, HB, 1), F32),
                 jax.ShapeDtypeStruct((sq.Bs, HG, HB, 1), F32)]
    return pl.pallas_call(
        functools.partial(_mlstm_kernel, HB=HB, dk=dk, dv=dv, sq=sq),
        grid=(HG, sq.NG),
        in_specs=in_specs,
        out_specs=out_specs,
        out_shape=out_shape,
        scratch_shapes=[pltpu.VMEM((HB, dk, dv), F32),
                        pltpu.VMEM((HB, dk), F32),
                        pltpu.VMEM((HB, 1), F32)],
        compiler_params=_params(("parallel", "arbitrary")),
    )(proj, proj, proj, proj, gates_col, gates_row, C0,
      n0.reshape(sq.Bs, HG, HB, dk), m0.reshape(sq.Bs, HG, HB, 1), norm_g.reshape(1, dv))


def _gates_col(g, n_kinds, HG, HB):
    M = g.shape[0]
    return g.reshape(M, n_kinds, HG, HB).transpose(2, 0, 1, 3).reshape(HG, M, n_kinds * HB)


def _gates_row(g, n_kinds, HG, HB):
    M = g.shape[0]
    g = g.reshape(M // CHUNK, CHUNK, n_kinds, HG, HB).transpose(0, 3, 2, 4, 1)
    return g.reshape(M // CHUNK, HG, n_kinds * HB, CHUNK)


def kernel(x_prompt, x_sample, state_gdn_S, state_gdn_conv, state_gla_S, state_mlstm_C, state_mlstm_n,
           state_mlstm_m, c_prompt, c_sample, ada_w, ada_b, norm_g, ffn_w_in, ffn_w_out, gdn_gla_w_in,
           gdn_gla_w_out, gdn_conv_w, gdn_A_log, gdn_dt_bias, gdn_norm_g, gla_gate_w2, gla_gate_b,
           gla_norm_g, mlstm_w_in, mlstm_w_out, mlstm_gate_b, mlstm_norm_g, final_norm_g):
    Bp, Tp, D = x_prompt.shape
    Bs, Ts, _ = x_sample.shape
    depth = ada_w.shape[0]
    assert Tp % CHUNK == 0 and Ts % CHUNK == 0 and norm_g.shape[1] == 3
    sq = _Seqs(Bp, Tp // CHUNK, Bs, Ts // CHUNK)
    Mp, Ms = sq.Mp, sq.Ms

    _, _, Hg, dkg, dvg = state_gdn_S.shape
    _, _, Hl, dkl, dvl = state_gla_S.shape
    _, _, Hm, dkm, dvm = state_mlstm_C.shape
    Wg, Wl, Wm = Hg * dvg, Hl * dvl, Hm * dvm
    R = gla_gate_w2.shape[1]
    npv = state_gdn_conv.shape[2]
    assert npv + 1 == gdn_conv_w.shape[1] and npv <= SUBLANE and npv <= CHUNK
    HBg = min(16, Hg)
    HBl = min(4, Hl)
    HBm = min(8, Hm)

    ffn_w_out_b = ffn_w_out.astype(BF16)
    gdn_gla_w_in_b = gdn_gla_w_in.astype(BF16)
    gdn_gla_w_out_b = gdn_gla_w_out.astype(BF16)
    mlstm_w_in_b = mlstm_w_in.astype(BF16)
    mlstm_w_out_b = mlstm_w_out.astype(BF16)

    modg = _ada_mod(jnp.concatenate([c_prompt, c_sample], axis=0), ada_w, ada_b, sq)

    def ffn(x, layer, f, s):
        h = _prenorm(x, norm_g[layer, s], modg, layer, s, sq)
        hh = _matmul_swiglu(h, ffn_w_in, (layer, f))
        return _matmul_residual(hh, ffn_w_out_b, (layer, f), x, modg, layer, s, 0.5, sq, 512)

    x = (x_prompt.reshape(Mp, D), x_sample.reshape(Ms, D))
    new_gS, new_conv, new_lS, new_C, new_n, new_m = [], [], [], [], [], []
    for layer in range(depth):
        x = ffn(x, layer, 0, 0)
        i = layer // 2
        h = _prenorm(x, norm_g[layer, 1], modg, layer, 1, sq)
        if layer % 2 == 0:
            nqk = Hg * dkg
            nA = 2 * nqk + 2 * Wg
            nB = 2 * Hl * dkl + 2 * Wl
            oB = nA + 2 * Hg
            wb = gdn_gla_w_in_b[i]
            w_gate = jnp.concatenate([wb[:, nA:oB], wb[:, oB + nB:]], axis=1)
            ngc = w_gate.shape[1]
            p1 = jnp.zeros((1, ngc), F32).at[0, :Hg].set(gdn_A_log[i].astype(F32))
            p2 = jnp.zeros((1, ngc), F32).at[0, :Hg].set(gdn_dt_bias[i].astype(F32))
            projA = _matmul(h, gdn_gla_w_in_b, (i,), nA, F32)
            projB = _matmul(h, wb[:, oB:oB + nB], (), nB, F32)
            gates = _gate_proj(h, w_gate, p1, p2, "gdn_gla", Hg)
            HGg = Hg // HBg
            gcol = _gates_col(gates[:, :2 * Hg], 2, HGg, HBg)
            grow = _gates_row(gates[:, :Hg], 1, HGg, HBg)
            wq, u, pk, el, nbq, nbk, nbv = _gdn_prep(
                projA, (0, nqk, 2 * nqk), gcol, grow, state_gdn_conv[i].astype(F32),
                gdn_conv_w[i].astype(F32), Hg, dkg, dvg, HBg, sq)
            buf = jnp.zeros((sq.M, Wg + Wl), BF16)
            buf, gS_p, gS_s = _gdn_scan(wq, u, pk, el, projA, 2 * nqk + Wg, state_gdn_S[i].astype(F32),
                                        gdn_norm_g[i].astype(F32), buf, Hg, dkg, dvg, sq)
            oq_b, ok_b, ov_b, or_b = 0, Hl * dkl, 2 * Hl * dkl, 2 * Hl * dkl + Wl
            buf, lS_p, lS_s = _gla(
                projB, (oq_b, ok_b, ov_b, or_b), gates[:, 2 * Hg:], gla_gate_w2[i].astype(F32),
                gla_gate_b[i].astype(F32), state_gla_S[i].astype(F32), gla_norm_g[i].astype(F32),
                buf, Wg, Hl, dkl, dvl, HBl, sq)
            nb = jnp.concatenate([nbq, nbk, nbv], axis=-1)
            new_gS.append((gS_p, gS_s))
            new_conv.append((nb[:Bp], nb[Bp:]))
            new_lS.append((lS_p, lS_s))
            w_out, lead = gdn_gla_w_out_b, (i,)
        else:
            nqk = Hm * dkm
            nmain = 2 * nqk + 2 * Wm
            gb = mlstm_gate_b[i].astype(F32).reshape(1, 2 * Hm)
            proj = _matmul(h, mlstm_w_in_b, (i,), nmain, F32)
            gates = _gate_proj(h, mlstm_w_in_b[i][:, nmain:], gb, gb, "mlstm", Hm)
            HGm = Hm // HBm
            gcol = _gates_col(gates, 2, HGm, HBm)
            grow = _gates_row(gates, 2, HGm, HBm)
            buf, C_p, C_s, n_p, n_s, m_p, m_s = _mlstm(
                proj, (0, nqk, 2 * nqk, 2 * nqk + Wm), gcol, grow, state_mlstm_C[i].astype(F32),
                state_mlstm_n[i].astype(F32), state_mlstm_m[i].astype(F32),
                mlstm_norm_g[i].astype(F32), Hm, dkm, dvm, HBm, sq)
            new_C.append((C_p, C_s))
            new_n.append((n_p.reshape(Bp, Hm, dkm), n_s.reshape(Bs, Hm, dkm)))
            new_m.append((m_p.reshape(Bp, Hm), m_s.reshape(Bs, Hm)))
            w_out, lead = mlstm_w_out_b, (i,)
        x = _matmul_residual(buf, w_out, lead, x, modg, layer, 1, 1.0, sq, 1024)
        x = ffn(x, layer, 1, 2)

    y_p = _final_norm(x, final_norm_g, 0, Mp).reshape(Bp, Tp, D)
    y_s = _final_norm(x, final_norm_g, Mp, Ms).reshape(Bs, Ts, D)

    def pair(states):
        return jnp.stack([p for p, _ in states]), jnp.stack([s for _, s in states])

    gS_p, gS_s = pair(new_gS)
    gc_p, gc_s = pair(new_conv)
    lS_p, lS_s = pair(new_lS)
    C_p, C_s = pair(new_C)
    n_p, n_s = pair(new_n)
    m_p, m_s = pair(new_m)
    return (y_p, y_s, gS_p, gS_s, gc_p, gc_s, lS_p, lS_s, C_p, C_s, n_p, n_s, m_p, m_s)
```

```python
import functools

import numpy as np
import jax
import jax.numpy as jnp
from jax import lax
from jax.experimental import pallas as pl
from jax.experimental.pallas import tpu as pltpu

F32 = jnp.float32
BF16 = jnp.bfloat16
CHUNK = 64
EPS = 1e-6
GLA_TAU = 16.0
GLA_SUB = 16
V7X_VMEM_LIMIT = 58 * 1024 * 1024
LANE = 128
SUBLANE = 8
HI = lax.Precision.HIGHEST


def _pick(dim, target, align):
    best = None
    t = align
    while t <= min(dim, target):
        if dim % t == 0:
            best = t
        t += align
    return dim if best is None else best


def _sigmoid(x):
    return 1.0 / (1.0 + jnp.exp(-x))


def _log_sigmoid(x):
    return jnp.minimum(x, 0.0) - jnp.log1p(jnp.exp(-jnp.abs(x)))


def _softplus(x):
    return jnp.maximum(x, 0.0) + jnp.log1p(jnp.exp(-jnp.abs(x)))


def _dot(a, b):
    return jnp.dot(a.astype(BF16), b.astype(BF16), preferred_element_type=F32)


def _dot_nt(a, b):
    return lax.dot_general(a.astype(BF16), b.astype(BF16), (((1,), (1,)), ((), ())),
                           preferred_element_type=F32)


def _dot_tn(a, b):
    return lax.dot_general(a.astype(BF16), b.astype(BF16), (((0,), (0,)), ((), ())),
                           preferred_element_type=F32)


def _dot_hi(a, b):
    return jnp.dot(a, b, precision=HI, preferred_element_type=F32)


def _split_bf16(a):
    hi = a.astype(BF16)
    return hi, (a - hi.astype(F32)).astype(BF16)


def _dot_x3(a, b):
    (ah, al), (bh, bl) = a, b
    m = ah.shape[0]
    d = functools.partial(jnp.dot, preferred_element_type=F32)
    s = d(jnp.concatenate([ah, al], axis=0), bh)
    return s[:m] + (d(ah, bl) + s[m:])


def _params(sem):
    return pltpu.CompilerParams(dimension_semantics=sem, vmem_limit_bytes=V7X_VMEM_LIMIT)


def _chunk_pos(g, Bp, NCp, NCs):
    npc = Bp * NCp
    in_p = g < npc
    r = g - npc
    seq = jnp.where(in_p, g // NCp, Bp + r // NCs)
    cin = jnp.where(in_p, g % NCp, r % NCs)
    last = jnp.where(in_p, NCp - 1, NCs - 1)
    return seq, cin == 0, cin == last, in_p


def _tri_masks(L):
    ii = lax.broadcasted_iota(jnp.int32, (L, L), 0)
    jj = lax.broadcasted_iota(jnp.int32, (L, L), 1)
    return ii, jj


class _Seqs:
    def __init__(self, Bp, NCp, Bs, NCs):
        self.Bp, self.NCp, self.Bs, self.NCs = Bp, NCp, Bs, NCs
        self.Mp, self.Ms = Bp * NCp * CHUNK, Bs * NCs * CHUNK
        self.M = self.Mp + self.Ms
        self.NG = self.M // CHUNK
        self.Mg = int(np.gcd(self.Mp, self.Ms))

    def pos(self, g):
        return _chunk_pos(g, self.Bp, self.NCp, self.NCs)

    def seq_p(self, g):
        return jnp.minimum(self.pos(g)[0], self.Bp - 1)

    def seq_s(self, g):
        return jnp.maximum(self.pos(g)[0] - self.Bp, 0)


def _x_specs(x, tm, tn, sq, col_of):
    if isinstance(x, tuple):
        npt = sq.Mp // tm
        return [pl.BlockSpec((tm, tn), lambda i, *r: (jnp.minimum(i, npt - 1), col_of(*r))),
                pl.BlockSpec((tm, tn), lambda i, *r: (jnp.maximum(i - npt, 0), col_of(*r)))], list(x), npt
    return [pl.BlockSpec((tm, tn), lambda i, *r: (i, col_of(*r)))], [x], 0


def _on_source(x_refs, npt, body):
    if len(x_refs) == 1:
        body(x_refs[0])
    else:
        i = pl.program_id(0)
        pl.when(i < npt)(lambda: body(x_refs[0]))
        pl.when(i >= npt)(lambda: body(x_refs[1]))


def _ada_kernel(c_ref, w_ref, b_ref, o_ref, res_scr, *, sq):
    c = c_ref[...]
    sc = (c * _sigmoid(c)).astype(BF16)
    res_scr[...] = jnp.dot(sc, w_ref[0].astype(BF16), preferred_element_type=F32) + b_ref[0]
    tn = res_scr.shape[1]
    for b in range(sq.Bp):
        o_ref[0, b * sq.NCp:(b + 1) * sq.NCp, :] = jnp.broadcast_to(res_scr[b:b + 1, :], (sq.NCp, tn))
    base = sq.Bp * sq.NCp
    if sq.NCs == 1:
        o_ref[0, base:base + sq.Bs, :] = res_scr[sq.Bp:sq.Bp + sq.Bs, :]
    else:
        for b in range(sq.Bs):
            o_ref[0, base + b * sq.NCs:base + (b + 1) * sq.NCs, :] = jnp.broadcast_to(
                res_scr[sq.Bp + b:sq.Bp + b + 1, :], (sq.NCs, tn))


def _ada_mod(c, ada_w, ada_b, sq):
    depth, D, N = ada_w.shape
    Bt = c.shape[0]
    tn = _pick(N, 512, LANE)
    return pl.pallas_call(
        functools.partial(_ada_kernel, sq=sq),
        grid=(depth, N // tn),
        in_specs=[pl.BlockSpec((Bt, D), lambda l, j: (0, 0)),
                  pl.BlockSpec((1, D, tn), lambda l, j: (l, 0, j)),
                  pl.BlockSpec((1, 1, tn), lambda l, j: (l, 0, j))],
        out_specs=pl.BlockSpec((1, sq.NG, tn), lambda l, j: (l, 0, j)),
        out_shape=jax.ShapeDtypeStruct((depth, sq.NG, N), F32),
        scratch_shapes=[pltpu.VMEM((Bt, tn), F32)],
        compiler_params=_params(("parallel", "parallel")),
    )(c, ada_w, ada_b.reshape(depth, 1, N))


def _row_rms_scale(x_ref, rows):
    D = x_ref.shape[1]
    acc = None
    for c in range(0, D, LANE):
        xc = x_ref[rows, c:c + LANE]
        acc = xc * xc if acc is None else acc + xc * xc
    return lax.rsqrt(jnp.sum(acc, axis=-1, keepdims=True) * (1.0 / D) + EPS)


def _prenorm_kernel(*refs, npt):
    *x_refs, g_ref, sc_ref, sh_ref, o_ref = refs

    def body(x_ref):
        D = x_ref.shape[1]
        cw = _pick(D, 512, LANE)
        for r in range(x_ref.shape[0] // CHUNK):
            rows = slice(r * CHUNK, (r + 1) * CHUNK)
            rinv = _row_rms_scale(x_ref, rows)
            for c in range(0, D, cw):
                cols = slice(c, c + cw)
                gm = g_ref[:, cols] * (1.0 + sc_ref[r:r + 1, cols])
                o_ref[rows, cols] = ((x_ref[rows, cols] * rinv) * gm + sh_ref[r:r + 1, cols]).astype(o_ref.dtype)

    _on_source(x_refs, npt, body)


def _prenorm(x, g, modg, layer, s, sq):
    D = g.shape[0]
    tr = _pick(sq.Mg, 512, CHUNK * SUBLANE)
    ng = tr // CHUNK
    x_specs, xs, npt = _x_specs(x, tr, D, sq, lambda: 0)
    return pl.pallas_call(
        functools.partial(_prenorm_kernel, npt=npt),
        grid=(sq.M // tr,),
        in_specs=x_specs + [pl.BlockSpec((1, D), lambda i: (0, 0)),
                            pl.BlockSpec((None, ng, D), lambda i: (layer, i, s * 3 + 1)),
                            pl.BlockSpec((None, ng, D), lambda i: (layer, i, s * 3))],
        out_specs=pl.BlockSpec((tr, D), lambda i: (i, 0)),
        out_shape=jax.ShapeDtypeStruct((sq.M, D), BF16),
        compiler_params=_params(("parallel",)),
    )(*xs, g.reshape(1, D), modg, modg)


def _final_norm_kernel(x_ref, g_ref, o_ref):
    D = x_ref.shape[1]
    cw = _pick(D, 512, LANE)
    rt = _pick(x_ref.shape[0], CHUNK, SUBLANE)
    for r in range(0, x_ref.shape[0], rt):
        rows = slice(r, r + rt)
        rinv = _row_rms_scale(x_ref, rows)
        for c in range(0, D, cw):
            cols = slice(c, c + cw)
            o_ref[rows, cols] = (x_ref[rows, cols] * rinv) * g_ref[:, cols]


def _final_norm(x, g, row0, nrows):
    D = x.shape[1]
    tr = _pick(int(np.gcd(nrows, row0)) if row0 else nrows, 256, SUBLANE)
    assert row0 % tr == 0 and nrows % tr == 0
    off = row0 // tr
    return pl.pallas_call(
        _final_norm_kernel,
        grid=(nrows // tr,),
        in_specs=[pl.BlockSpec((tr, D), lambda i: (i + off, 0)),
                  pl.BlockSpec((1, D), lambda i: (0, 0))],
        out_specs=pl.BlockSpec((tr, D), lambda i: (i, 0)),
        out_shape=jax.ShapeDtypeStruct((nrows, D), F32),
        compiler_params=_params(("parallel",)),
    )(x, g.reshape(1, D))


def _w_spec(w, lead, K, tn, col_of):
    return pl.BlockSpec((None,) * len(lead) + (K, tn), lambda i, j: tuple(lead) + (0, col_of(j)))


def _mm_kernel(a_ref, w_ref, o_ref):
    o_ref[...] = jnp.dot(a_ref[...], w_ref[...], preferred_element_type=F32).astype(o_ref.dtype)


def _matmul(a, w, lead, ncols, out_dtype):
    M, K = a.shape
    tm = _pick(M, 1024, CHUNK)
    tn = _pick(ncols, 1024, LANE)
    return pl.pallas_call(
        _mm_kernel,
        grid=(M // tm, ncols // tn),
        in_specs=[pl.BlockSpec((tm, K), lambda i, j: (i, 0)),
                  _w_spec(w, lead, K, tn, lambda j: j)],
        out_specs=pl.BlockSpec((tm, tn), lambda i, j: (i, j)),
        out_shape=jax.ShapeDtypeStruct((M, ncols), out_dtype),
        compiler_params=_params(("parallel", "parallel")),
    )(a, w)


def _mm_swiglu_kernel(a_ref, wg_ref, wu_ref, o_ref):
    a = a_ref[...]
    g = jnp.dot(a, wg_ref[...].astype(BF16), preferred_element_type=F32)
    u = jnp.dot(a, wu_ref[...].astype(BF16), preferred_element_type=F32)
    rt = _pick(g.shape[0], 256, SUBLANE)
    for r in range(0, g.shape[0], rt):
        gr = g[r:r + rt, :]
        o_ref[r:r + rt, :] = ((gr * _sigmoid(gr)) * u[r:r + rt, :]).astype(o_ref.dtype)


def _matmul_swiglu(a, w, lead):
    M, K = a.shape
    F = w.shape[-1] // 2
    tm = _pick(M, 3072, CHUNK)
    tn = _pick(F, 256, LANE)
    nf = F // tn
    return pl.pallas_call(
        _mm_swiglu_kernel,
        grid=(M // tm, nf),
        in_specs=[pl.BlockSpec((tm, K), lambda i, j: (i, 0), pipeline_mode=pl.Buffered(1)),
                  _w_spec(w, lead, K, tn, lambda j: j),
                  _w_spec(w, lead, K, tn, lambda j: nf + j)],
        out_specs=pl.BlockSpec((tm, tn), lambda i, j: (i, j)),
        out_shape=jax.ShapeDtypeStruct((M, F), BF16),
        compiler_params=_params(("parallel", "parallel")),
    )(a, w, w)


def _mm_res_kernel(a_ref, w_ref, *refs, coef, npt):
    *x_refs, gate_ref, o_ref = refs
    y = jnp.dot(a_ref[...], w_ref[...], preferred_element_type=F32)

    def body(x_ref):
        for r in range(y.shape[0] // CHUNK):
            rows = slice(r * CHUNK, (r + 1) * CHUNK)
            o_ref[rows, :] = x_ref[rows, :] + (coef * gate_ref[r:r + 1, :]) * y[rows, :]

    _on_source(x_refs, npt, body)


def _matmul_residual(a, w, lead, x, modg, layer, s, coef, sq, tile):
    M, K = a.shape
    N = w.shape[-1]
    tm = _pick(sq.Mg, tile, CHUNK * SUBLANE)
    tn = _pick(N, tile, LANE)
    ng = tm // CHUNK
    gcol = (s * 3 + 2) * (N // tn)
    x_specs, xs, npt = _x_specs(x, tm, tn, sq, lambda j: j)
    return pl.pallas_call(
        functools.partial(_mm_res_kernel, coef=coef, npt=npt),
        grid=(M // tm, N // tn),
        in_specs=[pl.BlockSpec((tm, K), lambda i, j: (i, 0)),
                  _w_spec(w, lead, K, tn, lambda j: j)] + x_specs +
                 [pl.BlockSpec((None, ng, tn), lambda i, j: (layer, i, gcol + j))],
        out_specs=pl.BlockSpec((tm, tn), lambda i, j: (i, j)),
        out_shape=jax.ShapeDtypeStruct((M, N), F32),
        compiler_params=_params(("parallel", "parallel")),
    )(a, w, *xs, modg)


def _gate_proj_kernel(a_ref, w_ref, p1_ref, p2_ref, o_ref, *, kind, H):
    y = jnp.dot(a_ref[...], w_ref[...], preferred_element_type=F32)
    col = lax.broadcasted_iota(jnp.int32, y.shape, 1)
    if kind == "gdn_gla":
        loga = -jnp.exp(p1_ref[...]) * _softplus(y + p2_ref[...])
        o_ref[...] = jnp.where(col < H, loga, jnp.where(col < 2 * H, _sigmoid(y), y))
    else:
        z = y + p1_ref[...]
        o_ref[...] = jnp.where(col < H, z, _log_sigmoid(z))


def _gate_proj(a, w, p1, p2, kind, H):
    M, K = a.shape
    N = w.shape[1]
    tm = _pick(M, 1024, CHUNK)
    return pl.pallas_call(
        functools.partial(_gate_proj_kernel, kind=kind, H=H),
        grid=(M // tm,),
        in_specs=[pl.BlockSpec((tm, K), lambda i: (i, 0)),
                  pl.BlockSpec((K, N), lambda i: (0, 0)),
                  pl.BlockSpec((1, N), lambda i: (0, 0)),
                  pl.BlockSpec((1, N), lambda i: (0, 0))],
        out_specs=pl.BlockSpec((tm, N), lambda i: (i, 0)),
        out_shape=jax.ShapeDtypeStruct((M, N), F32),
        compiler_params=_params(("parallel",)),
    )(a, w, p1, p2)


def _gdn_prep_kernel(x_ref, p_ref, cs_ref, cw_ref, gcol_ref, grow_ref,
                     wq_ref, u_ref, pk_out_ref, el_ref, nb_ref, in_scr, cv_scr, *, HB, dk, dv, sq):
    L = CHUNK
    g = pl.program_id(1)
    _, first, last, in_p = sq.pos(g)
    Wc = cw_ref.shape[0]
    npv = Wc - 1
    HK = HB * dk
    ncol = x_ref.shape[1]

    in_scr[8 - npv:8, :] = jnp.where(first, jnp.where(in_p, 0.0, cs_ref[0]), p_ref[8 - npv:8, :])
    in_scr[8:8 + L, :] = x_ref[...]
    cwid = _pick(ncol, 512, LANE)
    for c in range(0, ncol, cwid):
        cols = slice(c, c + cwid)
        y = in_scr[8 - npv:8 - npv + L, cols] * cw_ref[0:1, cols]
        for t in range(1, Wc):
            y = y + in_scr[8 - npv + t:8 - npv + t + L, cols] * cw_ref[t:t + 1, cols]
        cv_scr[:, cols] = y * _sigmoid(y)

    ii, jj = _tri_masks(L)
    incl = jj <= ii
    strict = jj < ii
    eye = (ii == jj).astype(F32)
    eye_b = eye.astype(BF16)
    gcol = gcol_ref[0]
    gc_all = _dot_hi(incl.astype(F32), gcol[:, :HB])
    gr_all = _dot_hi(grow_ref[0, 0], (ii <= jj).astype(F32))
    n_sq = int(np.log2(L)) - 1

    heads = range(HB)
    sks = [slice(j * dk, (j + 1) * dk) for j in heads]
    svs = [slice(j * dv, (j + 1) * dv) for j in heads]
    kn, gc, beta, eg, gL, kb, Ps, T = ([None] * HB for _ in range(8))
    for j in heads:
        q, k = cv_scr[:, sks[j]], cv_scr[:, HK + j * dk:HK + (j + 1) * dk]
        qn = (q * lax.rsqrt(jnp.sum(q * q, axis=-1, keepdims=True) + EPS)) * (dk ** -0.5)
        kn[j] = k * lax.rsqrt(jnp.sum(k * k, axis=-1, keepdims=True) + EPS)
        gc[j] = gc_all[:, j:j + 1]
        beta[j] = gcol[:, HB + j:HB + j + 1]
        dec_i = jnp.exp(jnp.where(incl, gc[j] - gr_all[j:j + 1, :], -jnp.inf))
        eg[j] = jnp.exp(gc[j])
        gL[j] = gc[j][L - 1:L, :]
        kb[j] = kn[j].astype(BF16)
        qb = qn.astype(BF16)
        wq_ref[0, L:2 * L, sks[j]] = (eg[j] * qn).astype(BF16)
        pk_out_ref[0, j, 0:L, :] = (dec_i * _dot_nt(qb, kb[j])).astype(BF16)
        A = beta[j] * jnp.where(strict, dec_i, 0.0) * _dot_nt(kb[j], kb[j])
        T[j] = eye - A
        Ps[j] = _split_bf16(-A)
    for _ in range(n_sq):
        for j in heads:
            Ps[j] = _split_bf16(_dot_x3(Ps[j], Ps[j]))
        for j in heads:
            T[j] = T[j] + _dot_x3(_split_bf16(T[j]), Ps[j])
    for j in heads:
        v = cv_scr[:, 2 * HK + j * dv:2 * HK + (j + 1) * dv]
        rhs = jnp.concatenate([(beta[j] * eg[j]) * kn[j], beta[j] * v], axis=1)
        WU = _dot(T[j], rhs)
        wq_ref[0, 0:L, sks[j]] = WU[:, :dk].astype(BF16)
        u_ref[:, svs[j]] = WU[:, dk:]
        kw = (kn[j] * jnp.exp(gL[j] - gc[j])).astype(BF16)
        pk_out_ref[0, j, L:L + dk, :] = _dot_tn(kw, eye_b).astype(BF16)
        el_ref[0, :, svs[j]] = jnp.broadcast_to(jnp.exp(gL[j]), (SUBLANE, dv))

    @pl.when(last)
    def _():
        nb_ref[0] = x_ref[L - npv:L, :]


def _gdn_prep(proj, gates_col, gates_row, conv_state, conv_w, H, dk, dv, sq):
    assert dk == dv
    HK = H * dk
    W3 = 3 * HK
    Bt = sq.Bp + sq.Bs
    Wc = conv_w.shape[0]
    npv = Wc - 1
    in_specs = [pl.BlockSpec((CHUNK, W3), lambda h, g: (g, 0)),
                pl.BlockSpec((SUBLANE, W3), lambda h, g: (jnp.maximum(g * (CHUNK // SUBLANE) - 1, 0), 0)),
                pl.BlockSpec((1, npv, W3), lambda h, g: (sq.seq_s(g), 0, 0)),
                pl.BlockSpec((Wc, W3), lambda h, g: (0, 0)),
                pl.BlockSpec((1, CHUNK, 2 * H), lambda h, g: (0, g, 0)),
                pl.BlockSpec((1, 1, H, CHUNK), lambda h, g: (g, 0, 0, 0))]
    out_specs = [pl.BlockSpec((1, 2 * CHUNK, HK), lambda h, g: (g, 0, 0)),
                 pl.BlockSpec((CHUNK, HK), lambda h, g: (g, 0)),
                 pl.BlockSpec((1, H, CHUNK + dk, CHUNK), lambda h, g: (g, 0, 0, 0)),
                 pl.BlockSpec((1, SUBLANE, HK), lambda h, g: (g, 0, 0)),
                 pl.BlockSpec((1, npv, W3), lambda h, g: (sq.pos(g)[0], 0, 0))]
    out_shape = [jax.ShapeDtypeStruct((sq.NG, 2 * CHUNK, HK), BF16),
                 jax.ShapeDtypeStruct((sq.M, HK), F32),
                 jax.ShapeDtypeStruct((sq.NG, H, CHUNK + dk, CHUNK), BF16),
                 jax.ShapeDtypeStruct((sq.NG, SUBLANE, HK), F32),
                 jax.ShapeDtypeStruct((Bt, npv, W3), F32)]
    return pl.pallas_call(
        functools.partial(_gdn_prep_kernel, HB=H, dk=dk, dv=dv, sq=sq),
        grid=(1, sq.NG),
        in_specs=in_specs,
        out_specs=out_specs,
        out_shape=out_shape,
        scratch_shapes=[pltpu.VMEM((SUBLANE + CHUNK, W3), F32), pltpu.VMEM((CHUNK, W3), F32)],
        compiler_params=_params(("parallel", "arbitrary")),
    )(proj, proj, conv_state, conv_w, gates_col, gates_row)


def _gdn_scan_kernel(wq_ref, u_ref, pk_ref, el_ref, ga_ref, s0_ref, ng_ref, buf_ref,
                     o_ref, sp_ref, ss_ref, s_scr, *, H, dk, dv, sq):
    del buf_ref
    L = CHUNK
    g = pl.program_id(0)
    _, first, last, in_p = sq.pos(g)

    @pl.when(first & in_p)
    def _():
        s_scr[...] = jnp.zeros_like(s_scr)

    @pl.when(first & jnp.logical_not(in_p))
    def _():
        s_scr[...] = s0_ref[0]

    heads = range(H)
    sks = [slice(j * dk, (j + 1) * dk) for j in heads]
    svs = [slice(j * dv, (j + 1) * dv) for j in heads]
    M1, M2 = [None] * H, [None] * H
    for j in heads:
        M1[j] = jnp.dot(wq_ref[0, :, sks[j]], s_scr[j].astype(BF16), preferred_element_type=F32)
    for j in heads:
        E = u_ref[:, svs[j]] - M1[j][:L]
        M2[j] = jnp.dot(pk_ref[0, j], E.astype(BF16), preferred_element_type=F32)
    for j in heads:
        s_scr[j] = el_ref[0, 0:1, svs[j]] * s_scr[j] + M2[j][L:]
        o = M1[j][L:] + M2[j][:L]
        on = (o * lax.rsqrt(jnp.mean(o * o, axis=-1, keepdims=True) + EPS)) * ng_ref[...]
        ga = ga_ref[:, svs[j]]
        o_ref[:, svs[j]] = (on * (ga * _sigmoid(ga))).astype(o_ref.dtype)

    @pl.when(last & in_p)
    def _():
        sp_ref[0] = s_scr[...]

    @pl.when(last & jnp.logical_not(in_p))
    def _():
        ss_ref[0] = s_scr[...]


def _gdn_scan(wq, u, pk, el, proj, gate_off, S0, norm_g, buf, H, dk, dv, sq):
    HK, HV = H * dk, H * dv
    in_specs = [pl.BlockSpec((1, 2 * CHUNK, HK), lambda g: (g, 0, 0)),
                pl.BlockSpec((CHUNK, HV), lambda g: (g, 0)),
                pl.BlockSpec((1, H, CHUNK + dk, CHUNK), lambda g: (g, 0, 0, 0)),
                pl.BlockSpec((1, SUBLANE, HV), lambda g: (g, 0, 0)),
                pl.BlockSpec((CHUNK, HV), lambda g: (g, gate_off // HV)),
                pl.BlockSpec((1, H, dk, dv), lambda g: (sq.seq_s(g), 0, 0, 0)),
                pl.BlockSpec((1, dv), lambda g: (0, 0)),
                pl.BlockSpec(memory_space=pl.ANY)]
    out_specs = [pl.BlockSpec((CHUNK, HV), lambda g: (g, 0)),
                 pl.BlockSpec((1, H, dk, dv), lambda g: (sq.seq_p(g), 0, 0, 0)),
                 pl.BlockSpec((1, H, dk, dv), lambda g: (sq.seq_s(g), 0, 0, 0))]
    out_shape = [jax.ShapeDtypeStruct(buf.shape, buf.dtype),
                 jax.ShapeDtypeStruct((sq.Bp, H, dk, dv), F32),
                 jax.ShapeDtypeStruct((sq.Bs, H, dk, dv), F32)]
    return pl.pallas_call(
        functools.partial(_gdn_scan_kernel, H=H, dk=dk, dv=dv, sq=sq),
        grid=(sq.NG,),
        in_specs=in_specs,
        out_specs=out_specs,
        out_shape=out_shape,
        scratch_shapes=[pltpu.VMEM((H, dk, dv), F32)],
        input_output_aliases={7: 0},
        compiler_params=_params(("arbitrary",)),
    )(wq, u, pk, el, proj, S0, norm_g.reshape(1, dv), buf)


def _gla_kernel(x_ref, lr_ref, w2_ref, gb_ref, s0_ref, ng_ref, buf_ref,
                o_ref, sp_ref, ss_ref, s_scr, *, HB, dk, dv, sq):
    del buf_ref
    HK, HV = HB * dk, HB * dv
    q_ref, k_ref = x_ref.at[:, 0:HK], x_ref.at[:, HK:2 * HK]
    v_ref, r_ref = x_ref.at[:, 2 * HK:2 * HK + HV], x_ref.at[:, 2 * HK + HV:2 * HK + 2 * HV]
    L = CHUNK
    g = pl.program_id(1)
    _, first, last, in_p = sq.pos(g)

    @pl.when(first & in_p)
    def _():
        s_scr[...] = jnp.zeros_like(s_scr)

    @pl.when(first & jnp.logical_not(in_p))
    def _():
        s_scr[...] = s0_ref[0]

    ii, jj = _tri_masks(L)
    incl = jj <= ii
    tri = incl.astype(F32)
    krow = lax.broadcasted_iota(jnp.int32, (L, dk), 0)
    di = lax.broadcasted_iota(jnp.int32, (dk, dk), 0)
    dj = lax.broadcasted_iota(jnp.int32, (dk, dk), 1)
    lr = lr_ref[...]

    heads = range(HB)
    sks = [slice(j * dk, (j + 1) * dk) for j in heads]
    svs = [slice(j * dv, (j + 1) * dv) for j in heads]
    G, att = [None] * HB, [None] * HB
    for j in heads:
        x = _dot(lr, w2_ref[:, sks[j]]) + gb_ref[:, sks[j]]
        G[j] = _dot_hi(tri, _log_sigmoid(x) / GLA_TAU)
    for j in heads:
        q = q_ref[:, sks[j]] * (dk ** -0.5)
        k = k_ref[:, sks[j]]
        rows = []
        for b in range(L // GLA_SUB):
            lo, hi = b * GLA_SUB, (b + 1) * GLA_SUB
            mid = lo + GLA_SUB // 2
            ref_g = G[j][mid:mid + 1, :]
            qg = q[lo:hi, :] * jnp.exp(G[j][lo:hi, :] - ref_g)
            kg = k * jnp.exp(jnp.where(krow < hi, ref_g - G[j], -jnp.inf))
            rows.append(_dot_nt(qg, kg))
        att[j] = jnp.where(incl, jnp.concatenate(rows, axis=0), 0.0)
    for j in heads:
        q = q_ref[:, sks[j]] * (dk ** -0.5)
        o = _dot(q * jnp.exp(G[j]), s_scr[j]) + _dot(att[j], v_ref[:, svs[j]])
        on = (o * lax.rsqrt(jnp.mean(o * o, axis=-1, keepdims=True) + EPS)) * ng_ref[...]
        r = r_ref[:, svs[j]]
        o_ref[:, svs[j]] = (on * (r * _sigmoid(r))).astype(o_ref.dtype)
    for j in heads:
        GL = G[j][L - 1:L, :]
        kS = k_ref[:, sks[j]] * jnp.exp(GL - G[j])
        a_col = jnp.sum(jnp.where(di == dj, jnp.broadcast_to(jnp.exp(GL), (dk, dk)), 0.0),
                        axis=1, keepdims=True)
        s_scr[j] = a_col * s_scr[j] + _dot_tn(kS, v_ref[:, svs[j]])

    @pl.when(last & in_p)
    def _():
        sp_ref[0] = s_scr[...]

    @pl.when(last & jnp.logical_not(in_p))
    def _():
        ss_ref[0] = s_scr[...]


def _gla(proj, lr, w2, gb, S0, norm_g, buf, buf_col_off, H, dk, dv, sq):
    HG, HB = 1, H
    WK, WV = HB * dk, HB * dv
    R = lr.shape[1]
    assert proj.shape[1] == 2 * WK + 2 * WV and buf_col_off % WV == 0

    in_specs = [pl.BlockSpec((CHUNK, 2 * WK + 2 * WV), lambda h, g: (g, 0)),
                pl.BlockSpec((CHUNK, R), lambda h, g: (g, 0)),
                pl.BlockSpec((R, WK), lambda h, g: (0, h)),
                pl.BlockSpec((1, WK), lambda h, g: (0, h)),
                pl.BlockSpec((1, HB, dk, dv), lambda h, g: (sq.seq_s(g), h, 0, 0)),
                pl.BlockSpec((1, dv), lambda h, g: (0, 0)),
                pl.BlockSpec(memory_space=pl.ANY)]
    out_specs = [pl.BlockSpec((CHUNK, WV), lambda h, g: (g, buf_col_off // WV + h)),
                 pl.BlockSpec((1, HB, dk, dv), lambda h, g: (sq.seq_p(g), h, 0, 0)),
                 pl.BlockSpec((1, HB, dk, dv), lambda h, g: (sq.seq_s(g), h, 0, 0))]
    out_shape = [jax.ShapeDtypeStruct(buf.shape, buf.dtype),
                 jax.ShapeDtypeStruct((sq.Bp, H, dk, dv), F32),
                 jax.ShapeDtypeStruct((sq.Bs, H, dk, dv), F32)]
    return pl.pallas_call(
        functools.partial(_gla_kernel, HB=HB, dk=dk, dv=dv, sq=sq),
        grid=(HG, sq.NG),
        in_specs=in_specs,
        out_specs=out_specs,
        out_shape=out_shape,
        scratch_shapes=[pltpu.VMEM((HB, dk, dv), F32)],
        input_output_aliases={6: 0},
        compiler_params=_params(("parallel", "arbitrary")),
    )(proj, lr, w2, gb.reshape(1, -1), S0, norm_g.reshape(1, dv), buf)


def _mlstm_kernel(x_ref, gcol_ref, grow_ref, c0_ref, n0_ref, m0_ref, ng_ref,
                  o_ref, cp_ref, cs_ref, np_ref, ns_ref, mp_ref, ms_ref, c_scr, n_scr, m_scr,
                  *, HB, dk, dv, sq):
    HK, HV = HB * dk, HB * dv
    q_ref, k_ref = x_ref.at[:, 0:HK], x_ref.at[:, HK:2 * HK]
    v_ref, og_ref = x_ref.at[:, 2 * HK:2 * HK + HV], x_ref.at[:, 2 * HK + HV:2 * HK + 2 * HV]
    L = CHUNK
    g = pl.program_id(1)
    _, first, last, in_p = sq.pos(g)

    @pl.when(first & in_p)
    def _():
        c_scr[...] = jnp.zeros_like(c_scr)
        n_scr[...] = jnp.zeros_like(n_scr)
        m_scr[...] = jnp.zeros_like(m_scr)

    @pl.when(first & jnp.logical_not(in_p))
    def _():
        c_scr[...] = c0_ref[0]
        n_scr[...] = n0_ref[0, 0]
        m_scr[...] = m0_ref[0, 0]

    ii, jj = _tri_masks(L)
    incl = jj <= ii
    gcol = gcol_ref[0]
    grow = grow_ref[0, 0]
    bc_all = _dot_hi(incl.astype(F32), gcol[:, HB:])
    br_all = _dot_hi(grow[HB:, :], (ii <= jj).astype(F32))

    heads = range(HB)
    sks = [slice(j * dk, (j + 1) * dk) for j in heads]
    svs = [slice(j * dv, (j + 1) * dv) for j in heads]
    mi, w_inter, expD, wk, decay, qb, Wm = ([None] * HB for _ in range(7))
    for j in heads:
        bc = bc_all[:, j:j + 1]
        m_prev = m_scr[j:j + 1, :]
        Dlog = jnp.where(incl, bc - br_all[j:j + 1, :] + grow[j:j + 1, :], -jnp.inf)
        inter = bc + m_prev
        mi[j] = jnp.maximum(inter, jnp.max(Dlog, axis=-1, keepdims=True))
        w_inter[j] = jnp.exp(inter - mi[j])
        expD[j] = jnp.exp(Dlog - mi[j])
        mL = mi[j][L - 1:L, :]
        bL = bc[L - 1:L, :]
        wk[j] = jnp.exp(bL - bc + gcol[:, j:j + 1] - mL)
        decay[j] = jnp.exp(bL + m_prev - mL)
        m_scr[j:j + 1, :] = mL
    for j in heads:
        qb[j] = (q_ref[:, sks[j]] * (dk ** -0.5)).astype(BF16)
        Wm[j] = expD[j] * _dot_nt(qb[j], k_ref[:, sks[j]])
    for j in heads:
        q = q_ref[:, sks[j]] * (dk ** -0.5)
        num = w_inter[j] * _dot(qb[j], c_scr[j]) + _dot(Wm[j], v_ref[:, svs[j]])
        den = (w_inter[j] * jnp.sum(q * n_scr[j:j + 1, :], axis=-1, keepdims=True)
               + jnp.sum(Wm[j], axis=-1, keepdims=True))
        h = num / jnp.maximum(jnp.abs(den), jnp.exp(-mi[j]))
        hn = (h * lax.rsqrt(jnp.mean(h * h, axis=-1, keepdims=True) + EPS)) * ng_ref[...]
        o_ref[:, svs[j]] = (hn * _sigmoid(og_ref[:, svs[j]])).astype(o_ref.dtype)
    for j in heads:
        kw = k_ref[:, sks[j]] * wk[j]
        c_scr[j] = decay[j] * c_scr[j] + _dot_tn(kw, v_ref[:, svs[j]])
        n_scr[j:j + 1, :] = decay[j] * n_scr[j:j + 1, :] + jnp.sum(kw, axis=0, keepdims=True)

    @pl.when(last & in_p)
    def _():
        cp_ref[0] = c_scr[...]
        np_ref[0, 0] = n_scr[...]
        mp_ref[0, 0] = m_scr[...]

    @pl.when(last & jnp.logical_not(in_p))
    def _():
        cs_ref[0] = c_scr[...]
        ns_ref[0, 0] = n_scr[...]
        ms_ref[0, 0] = m_scr[...]


def _mlstm(proj, gates_col, gates_row, C0, n0, m0, norm_g, H, dk, dv, sq):
    HG, HB = 1, H
    WK, WV = HB * dk, HB * dv
    assert proj.shape[1] == 2 * WK + 2 * WV

    def st(shape, seq_of):
        return pl.BlockSpec((1,) + shape, lambda h, g: (seq_of(g), h) + (0,) * (len(shape) - 1))

    in_specs = [pl.BlockSpec((CHUNK, 2 * WK + 2 * WV), lambda h, g: (g, 0)),
                pl.BlockSpec((1, CHUNK, 2 * HB), lambda h, g: (h, g, 0)),
                pl.BlockSpec((1, 1, 2 * HB, CHUNK), lambda h, g: (g, h, 0, 0)),
                st((HB, dk, dv), sq.seq_s), st((1, HB, dk), sq.seq_s), st((1, HB, 1), sq.seq_s),
                pl.BlockSpec((1, dv), lambda h, g: (0, 0))]
    out_specs = [pl.BlockSpec((CHUNK, WV), lambda h, g: (g, h)),
                 st((HB, dk, dv), sq.seq_p), st((HB, dk, dv), sq.seq_s),
                 st((1, HB, dk), sq.seq_p), st((1, HB, dk), sq.seq_s),
                 st((1, HB, 1), sq.seq_p), st((1, HB, 1), sq.seq_s)]
    out_shape = [jax.ShapeDtypeStruct((sq.M, H * dv), BF16),
                 jax.ShapeDtypeStruct((sq.Bp, H, dk, dv), F32),
                 jax.ShapeDtypeStruct((sq.Bs, H, dk, dv), F32),
                 jax.ShapeDtypeStruct((sq.Bp, HG, HB, dk), F32),
                 jax.ShapeDtypeStruct((sq.Bs, HG, HB, dk), F32),
                 jax.ShapeDtypeStruct((sq.Bp, HG, HB, 1), F32),
                 jax.ShapeDtypeStruct((sq.Bs, HG, HB, 1), F32)]
    return pl.pallas_call(
        functools.partial(_mlstm_kernel, HB=HB, dk=dk, dv=dv, sq=sq),
        grid=(HG, sq.NG),
        in_specs=in_specs,
        out_specs=out_specs,
        out_shape=out_shape,
        scratch_shapes=[pltpu.VMEM((HB, dk, dv), F32),
                        pltpu.VMEM((HB, dk), F32),
                        pltpu.VMEM((HB, 1), F32)],
        compiler_params=_params(("parallel", "arbitrary")),
    )(proj, gates_col, gates_row, C0,
      n0.reshape(sq.Bs, HG, HB, dk), m0.reshape(sq.Bs, HG, HB, 1), norm_g.reshape(1, dv))


def _gates_col(g, n_kinds, HG, HB):
    M = g.shape[0]
    return g.reshape(M, n_kinds, HG, HB).transpose(2, 0, 1, 3).reshape(HG, M, n_kinds * HB)


def _gates_row(g, n_kinds, HG, HB):
    M = g.shape[0]
    g = g.reshape(M // CHUNK, CHUNK, n_kinds, HG, HB).transpose(0, 3, 2, 4, 1)
    return g.reshape(M // CHUNK, HG, n_kinds * HB, CHUNK)


def kernel(x_prompt, x_sample, state_gdn_S, state_gdn_conv, state_gla_S, state_mlstm_C, state_mlstm_n,
           state_mlstm_m, c_prompt, c_sample, ada_w, ada_b, norm_g, ffn_w_in, ffn_w_out, gdn_gla_w_in,
           gdn_gla_w_out, gdn_conv_w, gdn_A_log, gdn_dt_bias, gdn_norm_g, gla_gate_w2, gla_gate_b,
           gla_norm_g, mlstm_w_in, mlstm_w_out, mlstm_gate_b, mlstm_norm_g, final_norm_g):
    Bp, Tp, D = x_prompt.shape
    Bs, Ts, _ = x_sample.shape
    depth = ada_w.shape[0]
    assert Tp % CHUNK == 0 and Ts % CHUNK == 0 and norm_g.shape[1] == 3
    sq = _Seqs(Bp, Tp // CHUNK, Bs, Ts // CHUNK)
    Mp, Ms = sq.Mp, sq.Ms

    _, _, Hg, dkg, dvg = state_gdn_S.shape
    _, _, Hl, dkl, dvl = state_gla_S.shape
    _, _, Hm, dkm, dvm = state_mlstm_C.shape
    Wg, Wl, Wm = Hg * dvg, Hl * dvl, Hm * dvm
    R = gla_gate_w2.shape[1]
    npv = state_gdn_conv.shape[2]
    assert npv + 1 == gdn_conv_w.shape[1] and npv <= SUBLANE and npv <= CHUNK

    ffn_w_out_b = ffn_w_out.astype(BF16)
    gdn_gla_w_in_b = gdn_gla_w_in.astype(BF16)
    gdn_gla_w_out_b = gdn_gla_w_out.astype(BF16)
    mlstm_w_in_b = mlstm_w_in.astype(BF16)
    mlstm_w_out_b = mlstm_w_out.astype(BF16)

    modg = _ada_mod(jnp.concatenate([c_prompt, c_sample], axis=0), ada_w, ada_b, sq)

    def ffn(x, layer, f, s):
        h = _prenorm(x, norm_g[layer, s], modg, layer, s, sq)
        hh = _matmul_swiglu(h, ffn_w_in, (layer, f))
        return _matmul_residual(hh, ffn_w_out_b, (layer, f), x, modg, layer, s, 0.5, sq, 512)

    x = (x_prompt.reshape(Mp, D), x_sample.reshape(Ms, D))
    new_gS, new_conv, new_lS, new_C, new_n, new_m = [], [], [], [], [], []
    for layer in range(depth):
        x = ffn(x, layer, 0, 0)
        i = layer // 2
        h = _prenorm(x, norm_g[layer, 1], modg, layer, 1, sq)
        if layer % 2 == 0:
            nqk = Hg * dkg
            nA = 2 * nqk + 2 * Wg
            nB = 2 * Hl * dkl + 2 * Wl
            oB = nA + 2 * Hg
            wb = gdn_gla_w_in_b[i]
            w_gate = jnp.concatenate([wb[:, nA:oB], wb[:, oB + nB:]], axis=1)
            ngc = w_gate.shape[1]
            p1 = jnp.zeros((1, ngc), F32).at[0, :Hg].set(gdn_A_log[i].astype(F32))
            p2 = jnp.zeros((1, ngc), F32).at[0, :Hg].set(gdn_dt_bias[i].astype(F32))
            projA = _matmul(h, gdn_gla_w_in_b, (i,), nA, F32)
            projB = _matmul(h, wb[:, oB:oB + nB], (), nB, F32)
            gates = _gate_proj(h, w_gate, p1, p2, "gdn_gla", Hg)
            gcol = _gates_col(gates[:, :2 * Hg], 2, 1, Hg)
            grow = _gates_row(gates[:, :Hg], 1, 1, Hg)
            wq, u, pk, el, nb = _gdn_prep(
                projA, gcol, grow, state_gdn_conv[i].astype(F32), gdn_conv_w[i].astype(F32),
                Hg, dkg, dvg, sq)
            buf = jnp.zeros((sq.M, Wg + Wl), BF16)
            buf, gS_p, gS_s = _gdn_scan(wq, u, pk, el, projA, 2 * nqk + Wg, state_gdn_S[i].astype(F32),
                                        gdn_norm_g[i].astype(F32), buf, Hg, dkg, dvg, sq)
            buf, lS_p, lS_s = _gla(
                projB, gates[:, 2 * Hg:], gla_gate_w2[i].astype(F32), gla_gate_b[i].astype(F32),
                state_gla_S[i].astype(F32), gla_norm_g[i].astype(F32), buf, Wg, Hl, dkl, dvl, sq)
            new_gS.append((gS_p, gS_s))
            new_conv.append((nb[:Bp], nb[Bp:]))
            new_lS.append((lS_p, lS_s))
            w_out, lead = gdn_gla_w_out_b, (i,)
        else:
            nqk = Hm * dkm
            nmain = 2 * nqk + 2 * Wm
            gb = mlstm_gate_b[i].astype(F32).reshape(1, 2 * Hm)
            proj = _matmul(h, mlstm_w_in_b, (i,), nmain, F32)
            gates = _gate_proj(h, mlstm_w_in_b[i][:, nmain:], gb, gb, "mlstm", Hm)
            gcol = _gates_col(gates, 2, 1, Hm)
            grow = _gates_row(gates, 2, 1, Hm)
            buf, C_p, C_s, n_p, n_s, m_p, m_s = _mlstm(
                proj, gcol, grow, state_mlstm_C[i].astype(F32), state_mlstm_n[i].astype(F32),
                state_mlstm_m[i].astype(F32), mlstm_norm_g[i].astype(F32), Hm, dkm, dvm, sq)
            new_C.append((C_p, C_s))
            new_n.append((n_p.reshape(Bp, Hm, dkm), n_s.reshape(Bs, Hm, dkm)))
            new_m.append((m_p.reshape(Bp, Hm), m_s.reshape(Bs, Hm)))
            w_out, lead = mlstm_w_out_b, (i,)
        x = _matmul_residual(buf, w_out, lead, x, modg, layer, 1, 1.0, sq, 1024)
        x = ffn(x, layer, 1, 2)

    y_p = _final_norm(x, final_norm_g, 0, Mp).reshape(Bp, Tp, D)
    y_s = _final_norm(x, final_norm_g, Mp, Ms).reshape(Bs, Ts, D)

    def pair(states):
        return jnp.stack([p for p, _ in states]), jnp.stack([s for _, s in states])

    gS_p, gS_s = pair(new_gS)
    gc_p, gc_s = pair(new_conv)
    lS_p, lS_s = pair(new_lS)
    C_p, C_s = pair(new_C)
    n_p, n_s = pair(new_n)
    m_p, m_s = pair(new_m)
    return (y_p, y_s, gS_p, gS_s, gc_p, gc_s, lS_p, lS_s, C_p, C_s, n_p, n_s, m_p, m_s)
```

```python
import functools

import numpy as np
import jax
import jax.numpy as jnp
from jax import lax
from jax.experimental import pallas as pl
from jax.experimental.pallas import tpu as pltpu

F32 = jnp.float32
BF16 = jnp.bfloat16
CHUNK = 64
EPS = 1e-6
GLA_TAU = 16.0
GLA_SUB = 16
V7X_VMEM_LIMIT = 58 * 1024 * 1024
LANE = 128
SUBLANE = 8
HI = lax.Precision.HIGHEST


def _pick(dim, target, align):
    best = None
    t = align
    while t <= min(dim, target):
        if dim % t == 0:
            best = t
        t += align
    return dim if best is None else best


def _sigmoid(x):
    return 1.0 / (1.0 + jnp.exp(-x))


def _log_sigmoid(x):
    return jnp.minimum(x, 0.0) - jnp.log1p(jnp.exp(-jnp.abs(x)))


def _softplus(x):
    return jnp.maximum(x, 0.0) + jnp.log1p(jnp.exp(-jnp.abs(x)))


def _dot(a, b):
    return jnp.dot(a.astype(BF16), b.astype(BF16), preferred_element_type=F32)


def _dot_nt(a, b):
    return lax.dot_general(a.astype(BF16), b.astype(BF16), (((1,), (1,)), ((), ())),
                           preferred_element_type=F32)


def _dot_tn(a, b):
    return lax.dot_general(a.astype(BF16), b.astype(BF16), (((0,), (0,)), ((), ())),
                           preferred_element_type=F32)


def _dot_hi(a, b):
    return jnp.dot(a, b, precision=HI, preferred_element_type=F32)


def _split_bf16(a):
    hi = a.astype(BF16)
    return hi, (a - hi.astype(F32)).astype(BF16)


def _dot_x3(a, b):
    (ah, al), (bh, bl) = a, b
    m = ah.shape[0]
    d = functools.partial(jnp.dot, preferred_element_type=F32)
    s = d(jnp.concatenate([ah, al], axis=0), bh)
    return s[:m] + (d(ah, bl) + s[m:])


def _params(sem):
    return pltpu.CompilerParams(dimension_semantics=sem, vmem_limit_bytes=V7X_VMEM_LIMIT)


def _chunk_pos(g, Bp, NCp, NCs):
    npc = Bp * NCp
    in_p = g < npc
    r = g - npc
    seq = jnp.where(in_p, g // NCp, Bp + r // NCs)
    cin = jnp.where(in_p, g % NCp, r % NCs)
    last = jnp.where(in_p, NCp - 1, NCs - 1)
    return seq, cin == 0, cin == last, in_p


def _tri_masks(L):
    ii = lax.broadcasted_iota(jnp.int32, (L, L), 0)
    jj = lax.broadcasted_iota(jnp.int32, (L, L), 1)
    return ii, jj


class _Seqs:
    def __init__(self, Bp, NCp, Bs, NCs):
        self.Bp, self.NCp, self.Bs, self.NCs = Bp, NCp, Bs, NCs
        self.Mp, self.Ms = Bp * NCp * CHUNK, Bs * NCs * CHUNK
        self.M = self.Mp + self.Ms
        self.NG = self.M // CHUNK
        self.Mg = int(np.gcd(self.Mp, self.Ms))

    def pos(self, g):
        return _chunk_pos(g, self.Bp, self.NCp, self.NCs)

    def seq_p(self, g):
        return jnp.minimum(self.pos(g)[0], self.Bp - 1)

    def seq_s(self, g):
        return jnp.maximum(self.pos(g)[0] - self.Bp, 0)


def _x_specs(x, tm, tn, sq, col_of):
    if isinstance(x, tuple):
        npt = sq.Mp // tm
        return [pl.BlockSpec((tm, tn), lambda i, *r: (jnp.minimum(i, npt - 1), col_of(*r))),
                pl.BlockSpec((tm, tn), lambda i, *r: (jnp.maximum(i - npt, 0), col_of(*r)))], list(x), npt
    return [pl.BlockSpec((tm, tn), lambda i, *r: (i, col_of(*r)))], [x], 0


def _on_source(x_refs, npt, body):
    if len(x_refs) == 1:
        body(x_refs[0])
    else:
        i = pl.program_id(0)
        pl.when(i < npt)(lambda: body(x_refs[0]))
        pl.when(i >= npt)(lambda: body(x_refs[1]))


def _ada_kernel(c_ref, w_ref, b_ref, o_ref, res_scr, *, sq):
    c = c_ref[...]
    sc = (c * _sigmoid(c)).astype(BF16)
    res_scr[...] = jnp.dot(sc, w_ref[0].astype(BF16), preferred_element_type=F32) + b_ref[0]
    tn = res_scr.shape[1]
    for b in range(sq.Bp):
        o_ref[0, b * sq.NCp:(b + 1) * sq.NCp, :] = jnp.broadcast_to(res_scr[b:b + 1, :], (sq.NCp, tn))
    base = sq.Bp * sq.NCp
    if sq.NCs == 1:
        o_ref[0, base:base + sq.Bs, :] = res_scr[sq.Bp:sq.Bp + sq.Bs, :]
    else:
        for b in range(sq.Bs):
            o_ref[0, base + b * sq.NCs:base + (b + 1) * sq.NCs, :] = jnp.broadcast_to(
                res_scr[sq.Bp + b:sq.Bp + b + 1, :], (sq.NCs, tn))


def _ada_mod(c, ada_w, ada_b, sq):
    depth, D, N = ada_w.shape
    Bt = c.shape[0]
    tn = _pick(N, 512, LANE)
    return pl.pallas_call(
        functools.partial(_ada_kernel, sq=sq),
        grid=(depth, N // tn),
        in_specs=[pl.BlockSpec((Bt, D), lambda l, j: (0, 0)),
                  pl.BlockSpec((1, D, tn), lambda l, j: (l, 0, j)),
                  pl.BlockSpec((1, 1, tn), lambda l, j: (l, 0, j))],
        out_specs=pl.BlockSpec((1, sq.NG, tn), lambda l, j: (l, 0, j)),
        out_shape=jax.ShapeDtypeStruct((depth, sq.NG, N), F32),
        scratch_shapes=[pltpu.VMEM((Bt, tn), F32)],
        compiler_params=_params(("parallel", "parallel")),
    )(c, ada_w, ada_b.reshape(depth, 1, N))


def _row_rms_scale(x_ref, rows):
    D = x_ref.shape[1]
    acc = None
    for c in range(0, D, LANE):
        xc = x_ref[rows, c:c + LANE]
        acc = xc * xc if acc is None else acc + xc * xc
    return lax.rsqrt(jnp.sum(acc, axis=-1, keepdims=True) * (1.0 / D) + EPS)


def _prenorm_kernel(*refs, npt):
    *x_refs, g_ref, sc_ref, sh_ref, o_ref = refs

    def body(x_ref):
        D = x_ref.shape[1]
        cw = _pick(D, 512, LANE)
        for r in range(x_ref.shape[0] // CHUNK):
            rows = slice(r * CHUNK, (r + 1) * CHUNK)
            rinv = _row_rms_scale(x_ref, rows)
            for c in range(0, D, cw):
                cols = slice(c, c + cw)
                gm = g_ref[:, cols] * (1.0 + sc_ref[r:r + 1, cols])
                o_ref[rows, cols] = ((x_ref[rows, cols] * rinv) * gm + sh_ref[r:r + 1, cols]).astype(o_ref.dtype)

    _on_source(x_refs, npt, body)


def _prenorm(x, g, modg, layer, s, sq):
    D = g.shape[0]
    tr = _pick(sq.Mg, 512, CHUNK * SUBLANE)
    ng = tr // CHUNK
    x_specs, xs, npt = _x_specs(x, tr, D, sq, lambda: 0)
    return pl.pallas_call(
        functools.partial(_prenorm_kernel, npt=npt),
        grid=(sq.M // tr,),
        in_specs=x_specs + [pl.BlockSpec((1, D), lambda i: (0, 0)),
                            pl.BlockSpec((None, ng, D), lambda i: (layer, i, s * 3 + 1)),
                            pl.BlockSpec((None, ng, D), lambda i: (layer, i, s * 3))],
        out_specs=pl.BlockSpec((tr, D), lambda i: (i, 0)),
        out_shape=jax.ShapeDtypeStruct((sq.M, D), BF16),
        compiler_params=_params(("parallel",)),
    )(*xs, g.reshape(1, D), modg, modg)


def _final_norm_kernel(x_ref, g_ref, o_ref):
    D = x_ref.shape[1]
    cw = _pick(D, 512, LANE)
    rt = _pick(x_ref.shape[0], CHUNK, SUBLANE)
    for r in range(0, x_ref.shape[0], rt):
        rows = slice(r, r + rt)
        rinv = _row_rms_scale(x_ref, rows)
        for c in range(0, D, cw):
            cols = slice(c, c + cw)
            o_ref[rows, cols] = (x_ref[rows, cols] * rinv) * g_ref[:, cols]


def _final_norm(x, g, row0, nrows):
    D = x.shape[1]
    tr = _pick(int(np.gcd(nrows, row0)) if row0 else nrows, 256, SUBLANE)
    assert row0 % tr == 0 and nrows % tr == 0
    off = row0 // tr
    return pl.pallas_call(
        _final_norm_kernel,
        grid=(nrows // tr,),
        in_specs=[pl.BlockSpec((tr, D), lambda i: (i + off, 0)),
                  pl.BlockSpec((1, D), lambda i: (0, 0))],
        out_specs=pl.BlockSpec((tr, D), lambda i: (i, 0)),
        out_shape=jax.ShapeDtypeStruct((nrows, D), F32),
        compiler_params=_params(("parallel",)),
    )(x, g.reshape(1, D))


def _w_spec(w, lead, K, tn, col_of):
    return pl.BlockSpec((None,) * len(lead) + (K, tn), lambda i, j: tuple(lead) + (0, col_of(j)))


def _mm_kernel(a_ref, w_ref, o_ref):
    o_ref[...] = jnp.dot(a_ref[...], w_ref[...].astype(BF16), preferred_element_type=F32).astype(o_ref.dtype)


def _matmul(a, w, lead, ncols, out_dtype):
    M, K = a.shape
    if w.dtype == F32:
        tm, tn, a_mode = _pick(M, 2048, CHUNK), _pick(ncols, 512, LANE), pl.Buffered(1)
    else:
        tm, tn, a_mode = _pick(M, 1024, CHUNK), _pick(ncols, 1024, LANE), None
    return pl.pallas_call(
        _mm_kernel,
        grid=(M // tm, ncols // tn),
        in_specs=[pl.BlockSpec((tm, K), lambda i, j: (i, 0), pipeline_mode=a_mode),
                  _w_spec(w, lead, K, tn, lambda j: j)],
        out_specs=pl.BlockSpec((tm, tn), lambda i, j: (i, j)),
        out_shape=jax.ShapeDtypeStruct((M, ncols), out_dtype),
        compiler_params=_params(("parallel", "parallel")),
    )(a, w)


def _mm_swiglu_kernel(a_ref, wg_ref, wu_ref, o_ref):
    a = a_ref[...]
    g = jnp.dot(a, wg_ref[...].astype(BF16), preferred_element_type=F32)
    u = jnp.dot(a, wu_ref[...].astype(BF16), preferred_element_type=F32)
    rt = _pick(g.shape[0], 256, SUBLANE)
    for r in range(0, g.shape[0], rt):
        gr = g[r:r + rt, :]
        o_ref[r:r + rt, :] = ((gr * _sigmoid(gr)) * u[r:r + rt, :]).astype(o_ref.dtype)


def _matmul_swiglu(a, w, lead):
    M, K = a.shape
    F = w.shape[-1] // 2
    tm = _pick(M, 3072, CHUNK)
    tn = _pick(F, 256, LANE)
    nf = F // tn
    return pl.pallas_call(
        _mm_swiglu_kernel,
        grid=(M // tm, nf),
        in_specs=[pl.BlockSpec((tm, K), lambda i, j: (i, 0), pipeline_mode=pl.Buffered(1)),
                  _w_spec(w, lead, K, tn, lambda j: j),
                  _w_spec(w, lead, K, tn, lambda j: nf + j)],
        out_specs=pl.BlockSpec((tm, tn), lambda i, j: (i, j)),
        out_shape=jax.ShapeDtypeStruct((M, F), BF16),
        compiler_params=_params(("parallel", "parallel")),
    )(a, w, w)


def _mm_res_kernel(a_ref, w_ref, *refs, coef, npt):
    *x_refs, gate_ref, o_ref = refs
    y = jnp.dot(a_ref[...], w_ref[...], preferred_element_type=F32)

    def body(x_ref):
        for r in range(y.shape[0] // CHUNK):
            rows = slice(r * CHUNK, (r + 1) * CHUNK)
            o_ref[rows, :] = x_ref[rows, :] + (coef * gate_ref[r:r + 1, :]) * y[rows, :]

    _on_source(x_refs, npt, body)


def _matmul_residual(a, w, lead, x, modg, layer, s, coef, sq, tile):
    M, K = a.shape
    N = w.shape[-1]
    tm = _pick(sq.Mg, tile, CHUNK * SUBLANE)
    tn = _pick(N, tile, LANE)
    ng = tm // CHUNK
    gcol = (s * 3 + 2) * (N // tn)
    x_specs, xs, npt = _x_specs(x, tm, tn, sq, lambda j: j)
    return pl.pallas_call(
        functools.partial(_mm_res_kernel, coef=coef, npt=npt),
        grid=(M // tm, N // tn),
        in_specs=[pl.BlockSpec((tm, K), lambda i, j: (i, 0)),
                  _w_spec(w, lead, K, tn, lambda j: j)] + x_specs +
                 [pl.BlockSpec((None, ng, tn), lambda i, j: (layer, i, gcol + j))],
        out_specs=pl.BlockSpec((tm, tn), lambda i, j: (i, j)),
        out_shape=jax.ShapeDtypeStruct((M, N), F32),
        compiler_params=_params(("parallel", "parallel")),
    )(a, w, *xs, modg)


def _gate_proj_kernel(a_ref, w_ref, p1_ref, p2_ref, o_ref, *, kind, H):
    y = jnp.dot(a_ref[...], w_ref[...], preferred_element_type=F32)
    col = lax.broadcasted_iota(jnp.int32, y.shape, 1)
    if kind == "gdn_gla":
        loga = -jnp.exp(p1_ref[...]) * _softplus(y + p2_ref[...])
        o_ref[...] = jnp.where(col < H, loga, jnp.where(col < 2 * H, _sigmoid(y), y))
    else:
        z = y + p1_ref[...]
        o_ref[...] = jnp.where(col < H, z, _log_sigmoid(z))


def _gate_proj(a, w, p1, p2, kind, H):
    M, K = a.shape
    N = w.shape[1]
    tm = _pick(M, 1024, CHUNK)
    return pl.pallas_call(
        functools.partial(_gate_proj_kernel, kind=kind, H=H),
        grid=(M // tm,),
        in_specs=[pl.BlockSpec((tm, K), lambda i: (i, 0)),
                  pl.BlockSpec((K, N), lambda i: (0, 0)),
                  pl.BlockSpec((1, N), lambda i: (0, 0)),
                  pl.BlockSpec((1, N), lambda i: (0, 0))],
        out_specs=pl.BlockSpec((tm, N), lambda i: (i, 0)),
        out_shape=jax.ShapeDtypeStruct((M, N), F32),
        compiler_params=_params(("parallel",)),
    )(a, w, p1, p2)


def _gdn_prep_kernel(x_ref, p_ref, cs_ref, cw_ref, gcol_ref, grow_ref,
                     wq_ref, u_ref, pk_out_ref, el_ref, nb_ref, in_scr, cv_scr, *, HB, dk, dv, sq):
    L = CHUNK
    g = pl.program_id(1)
    _, first, last, in_p = sq.pos(g)
    Wc = cw_ref.shape[0]
    npv = Wc - 1
    HK = HB * dk
    ncol = x_ref.shape[1]

    in_scr[8 - npv:8, :] = jnp.where(first, jnp.where(in_p, 0.0, cs_ref[0]), p_ref[8 - npv:8, :])
    in_scr[8:8 + L, :] = x_ref[...]
    cwid = _pick(ncol, 512, LANE)
    for c in range(0, ncol, cwid):
        cols = slice(c, c + cwid)
        y = in_scr[8 - npv:8 - npv + L, cols] * cw_ref[0:1, cols]
        for t in range(1, Wc):
            y = y + in_scr[8 - npv + t:8 - npv + t + L, cols] * cw_ref[t:t + 1, cols]
        cv_scr[:, cols] = y * _sigmoid(y)

    ii, jj = _tri_masks(L)
    incl = jj <= ii
    strict = jj < ii
    eye = (ii == jj).astype(F32)
    eye_b = eye.astype(BF16)
    gcol = gcol_ref[0]
    gc_all = _dot_hi(incl.astype(F32), gcol[:, :HB])
    gr_all = _dot_hi(grow_ref[0, 0], (ii <= jj).astype(F32))
    n_sq = int(np.log2(L)) - 1

    heads = range(HB)
    sks = [slice(j * dk, (j + 1) * dk) for j in heads]
    svs = [slice(j * dv, (j + 1) * dv) for j in heads]
    kn, gc, beta, eg, gL, kb, Ps, T = ([None] * HB for _ in range(8))
    for j in heads:
        q, k = cv_scr[:, sks[j]], cv_scr[:, HK + j * dk:HK + (j + 1) * dk]
        qn = (q * lax.rsqrt(jnp.sum(q * q, axis=-1, keepdims=True) + EPS)) * (dk ** -0.5)
        kn[j] = k * lax.rsqrt(jnp.sum(k * k, axis=-1, keepdims=True) + EPS)
        gc[j] = gc_all[:, j:j + 1]
        beta[j] = gcol[:, HB + j:HB + j + 1]
        dec_i = jnp.exp(jnp.where(incl, gc[j] - gr_all[j:j + 1, :], -jnp.inf))
        eg[j] = jnp.exp(gc[j])
        gL[j] = gc[j][L - 1:L, :]
        kb[j] = kn[j].astype(BF16)
        qb = qn.astype(BF16)
        wq_ref[0, L:2 * L, sks[j]] = (eg[j] * qn).astype(BF16)
        pk_out_ref[0, j, 0:L, :] = (dec_i * _dot_nt(qb, kb[j])).astype(BF16)
        A = beta[j] * jnp.where(strict, dec_i, 0.0) * _dot_nt(kb[j], kb[j])
        T[j] = eye - A
        Ps[j] = _split_bf16(-A)
    for _ in range(n_sq):
        for j in heads:
            Ps[j] = _split_bf16(_dot_x3(Ps[j], Ps[j]))
        for j in heads:
            T[j] = T[j] + _dot_x3(_split_bf16(T[j]), Ps[j])
    for j in heads:
        v = cv_scr[:, 2 * HK + j * dv:2 * HK + (j + 1) * dv]
        rhs = jnp.concatenate([(beta[j] * eg[j]) * kn[j], beta[j] * v], axis=1)
        WU = _dot(T[j], rhs)
        wq_ref[0, 0:L, sks[j]] = WU[:, :dk].astype(BF16)
        u_ref[:, svs[j]] = WU[:, dk:]
        kw = (kn[j] * jnp.exp(gL[j] - gc[j])).astype(BF16)
        pk_out_ref[0, j, L:L + dk, :] = _dot_tn(kw, eye_b).astype(BF16)
        el_ref[0, :, svs[j]] = jnp.broadcast_to(jnp.exp(gL[j]), (SUBLANE, dv))

    @pl.when(last)
    def _():
        nb_ref[0] = x_ref[L - npv:L, :]


def _gdn_prep(proj, gates_col, gates_row, conv_state, conv_w, H, dk, dv, sq):
    assert dk == dv
    HK = H * dk
    W3 = 3 * HK
    Bt = sq.Bp + sq.Bs
    Wc = conv_w.shape[0]
    npv = Wc - 1
    in_specs = [pl.BlockSpec((CHUNK, W3), lambda h, g: (g, 0)),
                pl.BlockSpec((SUBLANE, W3), lambda h, g: (jnp.maximum(g * (CHUNK // SUBLANE) - 1, 0), 0)),
                pl.BlockSpec((1, npv, W3), lambda h, g: (sq.seq_s(g), 0, 0)),
                pl.BlockSpec((Wc, W3), lambda h, g: (0, 0)),
                pl.BlockSpec((1, CHUNK, 2 * H), lambda h, g: (0, g, 0)),
                pl.BlockSpec((1, 1, H, CHUNK), lambda h, g: (g, 0, 0, 0))]
    out_specs = [pl.BlockSpec((1, 2 * CHUNK, HK), lambda h, g: (g, 0, 0)),
                 pl.BlockSpec((CHUNK, HK), lambda h, g: (g, 0)),
                 pl.BlockSpec((1, H, CHUNK + dk, CHUNK), lambda h, g: (g, 0, 0, 0)),
                 pl.BlockSpec((1, SUBLANE, HK), lambda h, g: (g, 0, 0)),
                 pl.BlockSpec((1, npv, W3), lambda h, g: (sq.pos(g)[0], 0, 0))]
    out_shape = [jax.ShapeDtypeStruct((sq.NG, 2 * CHUNK, HK), BF16),
                 jax.ShapeDtypeStruct((sq.M, HK), F32),
                 jax.ShapeDtypeStruct((sq.NG, H, CHUNK + dk, CHUNK), BF16),
                 jax.ShapeDtypeStruct((sq.NG, SUBLANE, HK), F32),
                 jax.ShapeDtypeStruct((Bt, npv, W3), F32)]
    return pl.pallas_call(
        functools.partial(_gdn_prep_kernel, HB=H, dk=dk, dv=dv, sq=sq),
        grid=(1, sq.NG),
        in_specs=in_specs,
        out_specs=out_specs,
        out_shape=out_shape,
        scratch_shapes=[pltpu.VMEM((SUBLANE + CHUNK, W3), F32), pltpu.VMEM((CHUNK, W3), F32)],
        compiler_params=_params(("parallel", "arbitrary")),
    )(proj, proj, conv_state, conv_w, gates_col, gates_row)


def _gdn_scan_kernel(wq_ref, u_ref, pk_ref, el_ref, ga_ref, s0_ref, ng_ref, buf_ref,
                     o_ref, sp_ref, ss_ref, s_scr, *, H, dk, dv, sq):
    del buf_ref
    L = CHUNK
    g = pl.program_id(0)
    _, first, last, in_p = sq.pos(g)

    @pl.when(first & in_p)
    def _():
        s_scr[...] = jnp.zeros_like(s_scr)

    @pl.when(first & jnp.logical_not(in_p))
    def _():
        s_scr[...] = s0_ref[0]

    heads = range(H)
    sks = [slice(j * dk, (j + 1) * dk) for j in heads]
    svs = [slice(j * dv, (j + 1) * dv) for j in heads]
    M1, M2 = [None] * H, [None] * H
    for j in heads:
        M1[j] = jnp.dot(wq_ref[0, :, sks[j]], s_scr[j].astype(BF16), preferred_element_type=F32)
    for j in heads:
        E = u_ref[:, svs[j]] - M1[j][:L]
        M2[j] = jnp.dot(pk_ref[0, j], E.astype(BF16), preferred_element_type=F32)
    for j in heads:
        s_scr[j] = el_ref[0, 0:1, svs[j]] * s_scr[j] + M2[j][L:]
        o = M1[j][L:] + M2[j][:L]
        on = (o * lax.rsqrt(jnp.mean(o * o, axis=-1, keepdims=True) + EPS)) * ng_ref[...]
        ga = ga_ref[:, svs[j]]
        o_ref[:, svs[j]] = (on * (ga * _sigmoid(ga))).astype(o_ref.dtype)

    @pl.when(last & in_p)
    def _():
        sp_ref[0] = s_scr[...]

    @pl.when(last & jnp.logical_not(in_p))
    def _():
        ss_ref[0] = s_scr[...]


def _gdn_scan(wq, u, pk, el, proj, gate_off, S0, norm_g, buf, H, dk, dv, sq):
    HK, HV = H * dk, H * dv
    in_specs = [pl.BlockSpec((1, 2 * CHUNK, HK), lambda g: (g, 0, 0)),
                pl.BlockSpec((CHUNK, HV), lambda g: (g, 0)),
                pl.BlockSpec((1, H, CHUNK + dk, CHUNK), lambda g: (g, 0, 0, 0)),
                pl.BlockSpec((1, SUBLANE, HV), lambda g: (g, 0, 0)),
                pl.BlockSpec((CHUNK, HV), lambda g: (g, gate_off // HV)),
                pl.BlockSpec((1, H, dk, dv), lambda g: (sq.seq_s(g), 0, 0, 0)),
                pl.BlockSpec((1, dv), lambda g: (0, 0)),
                pl.BlockSpec(memory_space=pl.ANY)]
    out_specs = [pl.BlockSpec((CHUNK, HV), lambda g: (g, 0)),
                 pl.BlockSpec((1, H, dk, dv), lambda g: (sq.seq_p(g), 0, 0, 0)),
                 pl.BlockSpec((1, H, dk, dv), lambda g: (sq.seq_s(g), 0, 0, 0))]
    out_shape = [jax.ShapeDtypeStruct(buf.shape, buf.dtype),
                 jax.ShapeDtypeStruct((sq.Bp, H, dk, dv), F32),
                 jax.ShapeDtypeStruct((sq.Bs, H, dk, dv), F32)]
    return pl.pallas_call(
        functools.partial(_gdn_scan_kernel, H=H, dk=dk, dv=dv, sq=sq),
        grid=(sq.NG,),
        in_specs=in_specs,
        out_specs=out_specs,
        out_shape=out_shape,
        scratch_shapes=[pltpu.VMEM((H, dk, dv), F32)],
        input_output_aliases={7: 0},
        compiler_params=_params(("arbitrary",)),
    )(wq, u, pk, el, proj, S0, norm_g.reshape(1, dv), buf)


def _gla_kernel(x_ref, lr_ref, w2_ref, gb_ref, s0_ref, ng_ref, buf_ref,
                o_ref, sp_ref, ss_ref, s_scr, *, HB, dk, dv, sq):
    del buf_ref
    HK, HV = HB * dk, HB * dv
    q_ref, k_ref = x_ref.at[:, 0:HK], x_ref.at[:, HK:2 * HK]
    v_ref, r_ref = x_ref.at[:, 2 * HK:2 * HK + HV], x_ref.at[:, 2 * HK + HV:2 * HK + 2 * HV]
    L = CHUNK
    g = pl.program_id(1)
    _, first, last, in_p = sq.pos(g)

    @pl.when(first & in_p)
    def _():
        s_scr[...] = jnp.zeros_like(s_scr)

    @pl.when(first & jnp.logical_not(in_p))
    def _():
        s_scr[...] = s0_ref[0]

    ii, jj = _tri_masks(L)
    incl = jj <= ii
    tri = incl.astype(F32)
    krow = lax.broadcasted_iota(jnp.int32, (L, dk), 0)
    di = lax.broadcasted_iota(jnp.int32, (dk, dk), 0)
    dj = lax.broadcasted_iota(jnp.int32, (dk, dk), 1)
    lr = lr_ref[...]

    heads = range(HB)
    sks = [slice(j * dk, (j + 1) * dk) for j in heads]
    svs = [slice(j * dv, (j + 1) * dv) for j in heads]
    G, att = [None] * HB, [None] * HB
    for j in heads:
        x = _dot(lr, w2_ref[:, sks[j]]) + gb_ref[:, sks[j]]
        G[j] = _dot_hi(tri, _log_sigmoid(x) / GLA_TAU)
    for j in heads:
        q = q_ref[:, sks[j]] * (dk ** -0.5)
        k = k_ref[:, sks[j]]
        rows = []
        for b in range(L // GLA_SUB):
            lo, hi = b * GLA_SUB, (b + 1) * GLA_SUB
            mid = lo + GLA_SUB // 2
            ref_g = G[j][mid:mid + 1, :]
            qg = q[lo:hi, :] * jnp.exp(G[j][lo:hi, :] - ref_g)
            kg = k * jnp.exp(jnp.where(krow < hi, ref_g - G[j], -jnp.inf))
            rows.append(_dot_nt(qg, kg))
        att[j] = jnp.where(incl, jnp.concatenate(rows, axis=0), 0.0)
    for j in heads:
        q = q_ref[:, sks[j]] * (dk ** -0.5)
        o = _dot(q * jnp.exp(G[j]), s_scr[j]) + _dot(att[j], v_ref[:, svs[j]])
        on = (o * lax.rsqrt(jnp.mean(o * o, axis=-1, keepdims=True) + EPS)) * ng_ref[...]
        r = r_ref[:, svs[j]]
        o_ref[:, svs[j]] = (on * (r * _sigmoid(r))).astype(o_ref.dtype)
    for j in heads:
        GL = G[j][L - 1:L, :]
        kS = k_ref[:, sks[j]] * jnp.exp(GL - G[j])
        a_col = jnp.sum(jnp.where(di == dj, jnp.broadcast_to(jnp.exp(GL), (dk, dk)), 0.0),
                        axis=1, keepdims=True)
        s_scr[j] = a_col * s_scr[j] + _dot_tn(kS, v_ref[:, svs[j]])

    @pl.when(last & in_p)
    def _():
        sp_ref[0] = s_scr[...]

    @pl.when(last & jnp.logical_not(in_p))
    def _():
        ss_ref[0] = s_scr[...]


def _gla(proj, lr, w2, gb, S0, norm_g, buf, buf_col_off, H, dk, dv, sq):
    HG, HB = 1, H
    WK, WV = HB * dk, HB * dv
    R = lr.shape[1]
    assert proj.shape[1] == 2 * WK + 2 * WV and buf_col_off % WV == 0

    in_specs = [pl.BlockSpec((CHUNK, 2 * WK + 2 * WV), lambda h, g: (g, 0)),
                pl.BlockSpec((CHUNK, R), lambda h, g: (g, 0)),
                pl.BlockSpec((R, WK), lambda h, g: (0, h)),
                pl.BlockSpec((1, WK), lambda h, g: (0, h)),
                pl.BlockSpec((1, HB, dk, dv), lambda h, g: (sq.seq_s(g), h, 0, 0)),
                pl.BlockSpec((1, dv), lambda h, g: (0, 0)),
                pl.BlockSpec(memory_space=pl.ANY)]
    out_specs = [pl.BlockSpec((CHUNK, WV), lambda h, g: (g, buf_col_off // WV + h)),
                 pl.BlockSpec((1, HB, dk, dv), lambda h, g: (sq.seq_p(g), h, 0, 0)),
                 pl.BlockSpec((1, HB, dk, dv), lambda h, g: (sq.seq_s(g), h, 0, 0))]
    out_shape = [jax.ShapeDtypeStruct(buf.shape, buf.dtype),
                 jax.ShapeDtypeStruct((sq.Bp, H, dk, dv), F32),
                 jax.ShapeDtypeStruct((sq.Bs, H, dk, dv), F32)]
    return pl.pallas_call(
        functools.partial(_gla_kernel, HB=HB, dk=dk, dv=dv, sq=sq),
        grid=(HG, sq.NG),
        in_specs=in_specs,
        out_specs=out_specs,
        out_shape=out_shape,
        scratch_shapes=[pltpu.VMEM((HB, dk, dv), F32)],
        input_output_aliases={6: 0},
        compiler_params=_params(("parallel", "arbitrary")),
    )(proj, lr, w2, gb.reshape(1, -1), S0, norm_g.reshape(1, dv), buf)


def _mlstm_kernel(x_ref, gcol_ref, grow_ref, c0_ref, n0_ref, m0_ref, ng_ref,
                  o_ref, cp_ref, cs_ref, np_ref, ns_ref, mp_ref, ms_ref, c_scr, n_scr, m_scr,
                  *, HB, dk, dv, sq):
    HK, HV = HB * dk, HB * dv
    q_ref, k_ref = x_ref.at[:, 0:HK], x_ref.at[:, HK:2 * HK]
    v_ref, og_ref = x_ref.at[:, 2 * HK:2 * HK + HV], x_ref.at[:, 2 * HK + HV:2 * HK + 2 * HV]
    L = CHUNK
    g = pl.program_id(1)
    _, first, last, in_p = sq.pos(g)

    @pl.when(first & in_p)
    def _():
        c_scr[...] = jnp.zeros_like(c_scr)
        n_scr[...] = jnp.zeros_like(n_scr)
        m_scr[...] = jnp.zeros_like(m_scr)

    @pl.when(first & jnp.logical_not(in_p))
    def _():
        c_scr[...] = c0_ref[0]
        n_scr[...] = n0_ref[0, 0]
        m_scr[...] = m0_ref[0, 0]

    ii, jj = _tri_masks(L)
    incl = jj <= ii
    gcol = gcol_ref[0]
    grow = grow_ref[0, 0]
    bc_all = _dot_hi(incl.astype(F32), gcol[:, HB:])
    br_all = _dot_hi(grow[HB:, :], (ii <= jj).astype(F32))

    heads = range(HB)
    sks = [slice(j * dk, (j + 1) * dk) for j in heads]
    svs = [slice(j * dv, (j + 1) * dv) for j in heads]
    mi, w_inter, expD, wk, decay, qb, Wm = ([None] * HB for _ in range(7))
    for j in heads:
        bc = bc_all[:, j:j + 1]
        m_prev = m_scr[j:j + 1, :]
        Dlog = jnp.where(incl, bc - br_all[j:j + 1, :] + grow[j:j + 1, :], -jnp.inf)
        inter = bc + m_prev
        mi[j] = jnp.maximum(inter, jnp.max(Dlog, axis=-1, keepdims=True))
        w_inter[j] = jnp.exp(inter - mi[j])
        expD[j] = jnp.exp(Dlog - mi[j])
        mL = mi[j][L - 1:L, :]
        bL = bc[L - 1:L, :]
        wk[j] = jnp.exp(bL - bc + gcol[:, j:j + 1] - mL)
        decay[j] = jnp.exp(bL + m_prev - mL)
        m_scr[j:j + 1, :] = mL
    for j in heads:
        qb[j] = (q_ref[:, sks[j]] * (dk ** -0.5)).astype(BF16)
        Wm[j] = expD[j] * _dot_nt(qb[j], k_ref[:, sks[j]])
    for j in heads:
        q = q_ref[:, sks[j]] * (dk ** -0.5)
        num = w_inter[j] * _dot(qb[j], c_scr[j]) + _dot(Wm[j], v_ref[:, svs[j]])
        den = (w_inter[j] * jnp.sum(q * n_scr[j:j + 1, :], axis=-1, keepdims=True)
               + jnp.sum(Wm[j], axis=-1, keepdims=True))
        h = num / jnp.maximum(jnp.abs(den), jnp.exp(-mi[j]))
        hn = (h * lax.rsqrt(jnp.mean(h * h, axis=-1, keepdims=True) + EPS)) * ng_ref[...]
        o_ref[:, svs[j]] = (hn * _sigmoid(og_ref[:, svs[j]])).astype(o_ref.dtype)
    for j in heads:
        kw = k_ref[:, sks[j]] * wk[j]
        c_scr[j] = decay[j] * c_scr[j] + _dot_tn(kw, v_ref[:, svs[j]])
        n_scr[j:j + 1, :] = decay[j] * n_scr[j:j + 1, :] + jnp.sum(kw, axis=0, keepdims=True)

    @pl.when(last & in_p)
    def _():
        cp_ref[0] = c_scr[...]
        np_ref[0, 0] = n_scr[...]
        mp_ref[0, 0] = m_scr[...]

    @pl.when(last & jnp.logical_not(in_p))
    def _():
        cs_ref[0] = c_scr[...]
        ns_ref[0, 0] = n_scr[...]
        ms_ref[0, 0] = m_scr[...]


def _mlstm(proj, gates_col, gates_row, C0, n0, m0, norm_g, H, dk, dv, sq):
    HG, HB = 1, H
    WK, WV = HB * dk, HB * dv
    assert proj.shape[1] == 2 * WK + 2 * WV

    def st(shape, seq_of):
        return pl.BlockSpec((1,) + shape, lambda h, g: (seq_of(g), h) + (0,) * (len(shape) - 1))

    in_specs = [pl.BlockSpec((CHUNK, 2 * WK + 2 * WV), lambda h, g: (g, 0)),
                pl.BlockSpec((1, CHUNK, 2 * HB), lambda h, g: (h, g, 0)),
                pl.BlockSpec((1, 1, 2 * HB, CHUNK), lambda h, g: (g, h, 0, 0)),
                st((HB, dk, dv), sq.seq_s), st((1, HB, dk), sq.seq_s), st((1, HB, 1), sq.seq_s),
                pl.BlockSpec((1, dv), lambda h, g: (0, 0))]
    out_specs = [pl.BlockSpec((CHUNK, WV), lambda h, g: (g, h)),
                 st((HB, dk, dv), sq.seq_p), st((HB, dk, dv), sq.seq_s),
                 st((1, HB, dk), sq.seq_p), st((1, HB, dk), sq.seq_s),
                 st((1, HB, 1), sq.seq_p), st((1, HB, 1), sq.seq_s)]
    out_shape = [jax.ShapeDtypeStruct((sq.M, H * dv), BF16),
                 jax.ShapeDtypeStruct((sq.Bp, H, dk, dv), F32),
                 jax.ShapeDtypeStruct((sq.Bs, H, dk, dv), F32),
                 jax.ShapeDtypeStruct((sq.Bp, HG, HB, dk), F32),
                 jax.ShapeDtypeStruct((sq.Bs, HG, HB, dk), F32),
                 jax.ShapeDtypeStruct((sq.Bp, HG, HB, 1), F32),
                 jax.ShapeDtypeStruct((sq.Bs, HG, HB, 1), F32)]
    return pl.pallas_call(
        functools.partial(_mlstm_kernel, HB=HB, dk=dk, dv=dv, sq=sq),
        grid=(HG, sq.NG),
        in_specs=in_specs,
        out_specs=out_specs,
        out_shape=out_shape,
        scratch_shapes=[pltpu.VMEM((HB, dk, dv), F32),
                        pltpu.VMEM((HB, dk), F32),
                        pltpu.VMEM((HB, 1), F32)],
        compiler_params=_params(("parallel", "arbitrary")),
    )(proj, gates_col, gates_row, C0,
      n0.reshape(sq.Bs, HG, HB, dk), m0.reshape(sq.Bs, HG, HB, 1), norm_g.reshape(1, dv))


def _gates_col(g, n_kinds, HG, HB):
    M = g.shape[0]
    return g.reshape(M, n_kinds, HG, HB).transpose(2, 0, 1, 3).reshape(HG, M, n_kinds * HB)


def _gates_row(g, n_kinds, HG, HB):
    M = g.shape[0]
    g = g.reshape(M // CHUNK, CHUNK, n_kinds, HG, HB).transpose(0, 3, 2, 4, 1)
    return g.reshape(M // CHUNK, HG, n_kinds * HB, CHUNK)


def kernel(x_prompt, x_sample, state_gdn_S, state_gdn_conv, state_gla_S, state_mlstm_C, state_mlstm_n,
           state_mlstm_m, c_prompt, c_sample, ada_w, ada_b, norm_g, ffn_w_in, ffn_w_out, gdn_gla_w_in,
           gdn_gla_w_out, gdn_conv_w, gdn_A_log, gdn_dt_bias, gdn_norm_g, gla_gate_w2, gla_gate_b,
           gla_norm_g, mlstm_w_in, mlstm_w_out, mlstm_gate_b, mlstm_norm_g, final_norm_g):
    Bp, Tp, D = x_prompt.shape
    Bs, Ts, _ = x_sample.shape
    depth = ada_w.shape[0]
    assert Tp % CHUNK == 0 and Ts % CHUNK == 0 and norm_g.shape[1] == 3
    sq = _Seqs(Bp, Tp // CHUNK, Bs, Ts // CHUNK)
    Mp, Ms = sq.Mp, sq.Ms

    _, _, Hg, dkg, dvg = state_gdn_S.shape
    _, _, Hl, dkl, dvl = state_gla_S.shape
    _, _, Hm, dkm, dvm = state_mlstm_C.shape
    Wg, Wl, Wm = Hg * dvg, Hl * dvl, Hm * dvm
    R = gla_gate_w2.shape[1]
    npv = state_gdn_conv.shape[2]
    assert npv + 1 == gdn_conv_w.shape[1] and npv <= SUBLANE and npv <= CHUNK

    ffn_w_out_b = ffn_w_out.astype(BF16)
    gdn_gla_w_out_b = gdn_gla_w_out.astype(BF16)
    mlstm_w_out_b = mlstm_w_out.astype(BF16)

    modg = _ada_mod(jnp.concatenate([c_prompt, c_sample], axis=0), ada_w, ada_b, sq)

    def ffn(x, layer, f, s):
        h = _prenorm(x, norm_g[layer, s], modg, layer, s, sq)
        hh = _matmul_swiglu(h, ffn_w_in, (layer, f))
        return _matmul_residual(hh, ffn_w_out_b, (layer, f), x, modg, layer, s, 0.5, sq, 512)

    x = (x_prompt.reshape(Mp, D), x_sample.reshape(Ms, D))
    new_gS, new_conv, new_lS, new_C, new_n, new_m = [], [], [], [], [], []
    for layer in range(depth):
        x = ffn(x, layer, 0, 0)
        i = layer // 2
        h = _prenorm(x, norm_g[layer, 1], modg, layer, 1, sq)
        if layer % 2 == 0:
            nqk = Hg * dkg
            nA = 2 * nqk + 2 * Wg
            nB = 2 * Hl * dkl + 2 * Wl
            oB = nA + 2 * Hg
            wb = gdn_gla_w_in[i]
            w_gate = jnp.concatenate([wb[:, nA:oB], wb[:, oB + nB:]], axis=1).astype(BF16)
            ngc = w_gate.shape[1]
            p1 = jnp.zeros((1, ngc), F32).at[0, :Hg].set(gdn_A_log[i].astype(F32))
            p2 = jnp.zeros((1, ngc), F32).at[0, :Hg].set(gdn_dt_bias[i].astype(F32))
            projA = _matmul(h, gdn_gla_w_in, (i,), nA, F32)
            projB = _matmul(h, wb[:, oB:oB + nB], (), nB, F32)
            gates = _gate_proj(h, w_gate, p1, p2, "gdn_gla", Hg)
            gcol = _gates_col(gates[:, :2 * Hg], 2, 1, Hg)
            grow = _gates_row(gates[:, :Hg], 1, 1, Hg)
            wq, u, pk, el, nb = _gdn_prep(
                projA, gcol, grow, state_gdn_conv[i].astype(F32), gdn_conv_w[i].astype(F32),
                Hg, dkg, dvg, sq)
            buf = jnp.zeros((sq.M, Wg + Wl), BF16)
            buf, gS_p, gS_s = _gdn_scan(wq, u, pk, el, projA, 2 * nqk + Wg, state_gdn_S[i].astype(F32),
                                        gdn_norm_g[i].astype(F32), buf, Hg, dkg, dvg, sq)
            buf, lS_p, lS_s = _gla(
                projB, gates[:, 2 * Hg:], gla_gate_w2[i].astype(F32), gla_gate_b[i].astype(F32),
                state_gla_S[i].astype(F32), gla_norm_g[i].astype(F32), buf, Wg, Hl, dkl, dvl, sq)
            new_gS.append((gS_p, gS_s))
            new_conv.append((nb[:Bp], nb[Bp:]))
            new_lS.append((lS_p, lS_s))
            w_out, lead = gdn_gla_w_out_b, (i,)
        else:
            nqk = Hm * dkm
            nmain = 2 * nqk + 2 * Wm
            gb = mlstm_gate_b[i].astype(F32).reshape(1, 2 * Hm)
            proj = _matmul(h, mlstm_w_in, (i,), nmain, F32)
            gates = _gate_proj(h, mlstm_w_in[i][:, nmain:].astype(BF16), gb, gb, "mlstm", Hm)
            gcol = _gates_col(gates, 2, 1, Hm)
            grow = _gates_row(gates, 2, 1, Hm)
            buf, C_p, C_s, n_p, n_s, m_p, m_s = _mlstm(
                proj, gcol, grow, state_mlstm_C[i].astype(F32), state_mlstm_n[i].astype(F32),
                state_mlstm_m[i].astype(F32), mlstm_norm_g[i].astype(F32), Hm, dkm, dvm, sq)
            new_C.append((C_p, C_s))
            new_n.append((n_p.reshape(Bp, Hm, dkm), n_s.reshape(Bs, Hm, dkm)))
            new_m.append((m_p.reshape(Bp, Hm), m_s.reshape(Bs, Hm)))
            w_out, lead = mlstm_w_out_b, (i,)
        x = _matmul_residual(buf, w_out, lead, x, modg, layer, 1, 1.0, sq, 1024)
        x = ffn(x, layer, 1, 2)

    y_p = _final_norm(x, final_norm_g, 0, Mp).reshape(Bp, Tp, D)
    y_s = _final_norm(x, final_norm_g, Mp, Ms).reshape(Bs, Ts, D)

    def pair(states):
        return jnp.stack([p for p, _ in states]), jnp.stack([s for _, s in states])

    gS_p, gS_s = pair(new_gS)
    gc_p, gc_s = pair(new_conv)
    lS_p, lS_s = pair(new_lS)
    C_p, C_s = pair(new_C)
    n_p, n_s = pair(new_n)
    m_p, m_s = pair(new_m)
    return (y_p, y_s, gS_p, gS_s, gc_p, gc_s, lS_p, lS_s, C_p, C_s, n_p, n_s, m_p, m_s)
```

```python
import functools

import numpy as np
import jax
import jax.numpy as jnp
from jax import lax
from jax.experimental import pallas as pl
from jax.experimental.pallas import tpu as pltpu

F32 = jnp.float32
BF16 = jnp.bfloat16
CHUNK = 64
EPS = 1e-6
GLA_TAU = 16.0
GLA_SUB = 16
V7X_VMEM_LIMIT = 58 * 1024 * 1024
LANE = 128
SUBLANE = 8
HI = lax.Precision.HIGHEST


def _pick(dim, target, align):
    best = None
    t = align
    while t <= min(dim, target):
        if dim % t == 0:
            best = t
        t += align
    return dim if best is None else best


def _sigmoid(x):
    return 1.0 / (1.0 + jnp.exp(-x))


def _log_sigmoid(x):
    return jnp.minimum(x, 0.0) - jnp.log1p(jnp.exp(-jnp.abs(x)))


def _softplus(x):
    return jnp.maximum(x, 0.0) + jnp.log1p(jnp.exp(-jnp.abs(x)))


def _dot(a, b):
    return jnp.dot(a.astype(BF16), b.astype(BF16), preferred_element_type=F32)


def _dot_nt(a, b):
    return lax.dot_general(a.astype(BF16), b.astype(BF16), (((1,), (1,)), ((), ())),
                           preferred_element_type=F32)


def _dot_tn(a, b):
    return lax.dot_general(a.astype(BF16), b.astype(BF16), (((0,), (0,)), ((), ())),
                           preferred_element_type=F32)


def _dot_hi(a, b):
    return jnp.dot(a, b, precision=HI, preferred_element_type=F32)


def _split_bf16(a):
    hi = a.astype(BF16)
    return hi, (a - hi.astype(F32)).astype(BF16)


def _dot_x3(a, b):
    (ah, al), (bh, bl) = a, b
    m = ah.shape[0]
    d = functools.partial(jnp.dot, preferred_element_type=F32)
    s = d(jnp.concatenate([ah, al], axis=0), bh)
    return s[:m] + (d(ah, bl) + s[m:])


def _params(sem):
    return pltpu.CompilerParams(dimension_semantics=sem, vmem_limit_bytes=V7X_VMEM_LIMIT)


def _chunk_pos(g, Bp, NCp, NCs):
    npc = Bp * NCp
    in_p = g < npc
    r = g - npc
    seq = jnp.where(in_p, g // NCp, Bp + r // NCs)
    cin = jnp.where(in_p, g % NCp, r % NCs)
    last = jnp.where(in_p, NCp - 1, NCs - 1)
    return seq, cin == 0, cin == last, in_p


def _tri_masks(L):
    ii = lax.broadcasted_iota(jnp.int32, (L, L), 0)
    jj = lax.broadcasted_iota(jnp.int32, (L, L), 1)
    return ii, jj


class _Seqs:
    def __init__(self, Bp, NCp, Bs, NCs):
        self.Bp, self.NCp, self.Bs, self.NCs = Bp, NCp, Bs, NCs
        self.Mp, self.Ms = Bp * NCp * CHUNK, Bs * NCs * CHUNK
        self.M = self.Mp + self.Ms
        self.NG = self.M // CHUNK
        self.Mg = int(np.gcd(self.Mp, self.Ms))

    def pos(self, g):
        return _chunk_pos(g, self.Bp, self.NCp, self.NCs)

    def seq_p(self, g):
        return jnp.minimum(self.pos(g)[0], self.Bp - 1)

    def seq_s(self, g):
        return jnp.maximum(self.pos(g)[0] - self.Bp, 0)


def _x_specs(x, tm, tn, sq, col_of):
    if isinstance(x, tuple):
        npt = sq.Mp // tm
        return [pl.BlockSpec((tm, tn), lambda i, *r: (jnp.minimum(i, npt - 1), col_of(*r))),
                pl.BlockSpec((tm, tn), lambda i, *r: (jnp.maximum(i - npt, 0), col_of(*r)))], list(x), npt
    return [pl.BlockSpec((tm, tn), lambda i, *r: (i, col_of(*r)))], [x], 0


def _on_source(x_refs, npt, body):
    if len(x_refs) == 1:
        body(x_refs[0])
    else:
        i = pl.program_id(0)
        pl.when(i < npt)(lambda: body(x_refs[0]))
        pl.when(i >= npt)(lambda: body(x_refs[1]))


def _ada_kernel(c_ref, w_ref, b_ref, o_ref, res_scr, *, sq):
    c = c_ref[...]
    sc = (c * _sigmoid(c)).astype(BF16)
    res_scr[...] = jnp.dot(sc, w_ref[0].astype(BF16), preferred_element_type=F32) + b_ref[0]
    tn = res_scr.shape[1]
    for b in range(sq.Bp):
        o_ref[0, b * sq.NCp:(b + 1) * sq.NCp, :] = jnp.broadcast_to(res_scr[b:b + 1, :], (sq.NCp, tn))
    base = sq.Bp * sq.NCp
    if sq.NCs == 1:
        o_ref[0, base:base + sq.Bs, :] = res_scr[sq.Bp:sq.Bp + sq.Bs, :]
    else:
        for b in range(sq.Bs):
            o_ref[0, base + b * sq.NCs:base + (b + 1) * sq.NCs, :] = jnp.broadcast_to(
                res_scr[sq.Bp + b:sq.Bp + b + 1, :], (sq.NCs, tn))


def _ada_mod(c, ada_w, ada_b, sq):
    depth, D, N = ada_w.shape
    Bt = c.shape[0]
    tn = _pick(N, 1024, LANE)
    return pl.pallas_call(
        functools.partial(_ada_kernel, sq=sq),
        grid=(depth, N // tn),
        in_specs=[pl.BlockSpec((Bt, D), lambda l, j: (0, 0)),
                  pl.BlockSpec((1, D, tn), lambda l, j: (l, 0, j)),
                  pl.BlockSpec((1, 1, tn), lambda l, j: (l, 0, j))],
        out_specs=pl.BlockSpec((1, sq.NG, tn), lambda l, j: (l, 0, j)),
        out_shape=jax.ShapeDtypeStruct((depth, sq.NG, N), F32),
        scratch_shapes=[pltpu.VMEM((Bt, tn), F32)],
        compiler_params=_params(("parallel", "parallel")),
    )(c, ada_w, ada_b.reshape(depth, 1, N))


def _row_rms_scale(x_ref, rows):
    D = x_ref.shape[1]
    acc = None
    for c in range(0, D, LANE):
        xc = x_ref[rows, c:c + LANE]
        acc = xc * xc if acc is None else acc + xc * xc
    return lax.rsqrt(jnp.sum(acc, axis=-1, keepdims=True) * (1.0 / D) + EPS)


def _prenorm_kernel(*refs, npt):
    *x_refs, g_ref, sc_ref, sh_ref, o_ref = refs

    def body(x_ref):
        D = x_ref.shape[1]
        cw = _pick(D, 512, LANE)
        for r in range(x_ref.shape[0] // CHUNK):
            rows = slice(r * CHUNK, (r + 1) * CHUNK)
            rinv = _row_rms_scale(x_ref, rows)
            for c in range(0, D, cw):
                cols = slice(c, c + cw)
                gm = g_ref[:, cols] * (1.0 + sc_ref[r:r + 1, cols])
                o_ref[rows, cols] = ((x_ref[rows, cols] * rinv) * gm + sh_ref[r:r + 1, cols]).astype(o_ref.dtype)

    _on_source(x_refs, npt, body)


def _prenorm(x, g, modg, layer, s, sq):
    D = g.shape[0]
    tr = _pick(sq.Mg, 512, CHUNK * SUBLANE)
    ng = tr // CHUNK
    x_specs, xs, npt = _x_specs(x, tr, D, sq, lambda: 0)
    return pl.pallas_call(
        functools.partial(_prenorm_kernel, npt=npt),
        grid=(sq.M // tr,),
        in_specs=x_specs + [pl.BlockSpec((1, D), lambda i: (0, 0)),
                            pl.BlockSpec((None, ng, D), lambda i: (layer, i, s * 3 + 1)),
                            pl.BlockSpec((None, ng, D), lambda i: (layer, i, s * 3))],
        out_specs=pl.BlockSpec((tr, D), lambda i: (i, 0)),
        out_shape=jax.ShapeDtypeStruct((sq.M, D), BF16),
        compiler_params=_params(("parallel",)),
    )(*xs, g.reshape(1, D), modg, modg)


def _final_norm_kernel(x_ref, g_ref, o_ref):
    D = x_ref.shape[1]
    cw = _pick(D, 512, LANE)
    rt = _pick(x_ref.shape[0], CHUNK, SUBLANE)
    for r in range(0, x_ref.shape[0], rt):
        rows = slice(r, r + rt)
        rinv = _row_rms_scale(x_ref, rows)
        for c in range(0, D, cw):
            cols = slice(c, c + cw)
            o_ref[rows, cols] = (x_ref[rows, cols] * rinv) * g_ref[:, cols]


def _final_norm(x, g, row0, nrows):
    D = x.shape[1]
    tr = _pick(int(np.gcd(nrows, row0)) if row0 else nrows, 256, SUBLANE)
    assert row0 % tr == 0 and nrows % tr == 0
    off = row0 // tr
    return pl.pallas_call(
        _final_norm_kernel,
        grid=(nrows // tr,),
        in_specs=[pl.BlockSpec((tr, D), lambda i: (i + off, 0)),
                  pl.BlockSpec((1, D), lambda i: (0, 0))],
        out_specs=pl.BlockSpec((tr, D), lambda i: (i, 0)),
        out_shape=jax.ShapeDtypeStruct((nrows, D), F32),
        compiler_params=_params(("parallel",)),
    )(x, g.reshape(1, D))


def _w_spec(w, lead, K, tn, col_of):
    return pl.BlockSpec((None,) * len(lead) + (K, tn), lambda i, j: tuple(lead) + (0, col_of(j)))


def _mm_kernel(a_ref, w_ref, o_ref):
    o_ref[...] = jnp.dot(a_ref[...], w_ref[...], preferred_element_type=F32).astype(o_ref.dtype)


def _matmul(a, w, lead, ncols, out_dtype):
    M, K = a.shape
    tm = _pick(M, 1024, CHUNK)
    tn = _pick(ncols, 1024, LANE)
    return pl.pallas_call(
        _mm_kernel,
        grid=(M // tm, ncols // tn),
        in_specs=[pl.BlockSpec((tm, K), lambda i, j: (i, 0)),
                  _w_spec(w, lead, K, tn, lambda j: j)],
        out_specs=pl.BlockSpec((tm, tn), lambda i, j: (i, j)),
        out_shape=jax.ShapeDtypeStruct((M, ncols), out_dtype),
        compiler_params=_params(("parallel", "parallel")),
    )(a, w)


def _mm_swiglu_kernel(a_ref, wg_ref, wu_ref, o_ref):
    wg = wg_ref[...].astype(BF16)
    wu = wu_ref[...].astype(BF16)
    rt = _pick(a_ref.shape[0], 768, CHUNK)
    for r in range(0, a_ref.shape[0], rt):
        a = a_ref[r:r + rt, :]
        g = jnp.dot(a, wg, preferred_element_type=F32)
        u = jnp.dot(a, wu, preferred_element_type=F32)
        o_ref[r:r + rt, :] = ((g * _sigmoid(g)) * u).astype(o_ref.dtype)


def _matmul_swiglu(a, w, lead):
    M, K = a.shape
    F = w.shape[-1] // 2
    tm = _pick(M, 3072, CHUNK)
    tn = _pick(F, 256, LANE)
    nf = F // tn
    return pl.pallas_call(
        _mm_swiglu_kernel,
        grid=(M // tm, nf),
        in_specs=[pl.BlockSpec((tm, K), lambda i, j: (i, 0), pipeline_mode=pl.Buffered(1)),
                  _w_spec(w, lead, K, tn, lambda j: j),
                  _w_spec(w, lead, K, tn, lambda j: nf + j)],
        out_specs=pl.BlockSpec((tm, tn), lambda i, j: (i, j)),
        out_shape=jax.ShapeDtypeStruct((M, F), BF16),
        compiler_params=_params(("parallel", "parallel")),
    )(a, w, w)


def _mm_res_kernel(a_ref, w_ref, *refs, coef, npt):
    *x_refs, gate_ref, o_ref = refs
    y = jnp.dot(a_ref[...], w_ref[...], preferred_element_type=F32)

    def body(x_ref):
        for r in range(y.shape[0] // CHUNK):
            rows = slice(r * CHUNK, (r + 1) * CHUNK)
            o_ref[rows, :] = x_ref[rows, :] + (coef * gate_ref[r:r + 1, :]) * y[rows, :]

    _on_source(x_refs, npt, body)


def _matmul_residual(a, w, lead, x, modg, layer, s, coef, sq, tile):
    M, K = a.shape
    N = w.shape[-1]
    tm = _pick(sq.Mg, tile, CHUNK * SUBLANE)
    tn = _pick(N, tile, LANE)
    ng = tm // CHUNK
    gcol = (s * 3 + 2) * (N // tn)
    x_specs, xs, npt = _x_specs(x, tm, tn, sq, lambda j: j)
    return pl.pallas_call(
        functools.partial(_mm_res_kernel, coef=coef, npt=npt),
        grid=(M // tm, N // tn),
        in_specs=[pl.BlockSpec((tm, K), lambda i, j: (i, 0)),
                  _w_spec(w, lead, K, tn, lambda j: j)] + x_specs +
                 [pl.BlockSpec((None, ng, tn), lambda i, j: (layer, i, gcol + j))],
        out_specs=pl.BlockSpec((tm, tn), lambda i, j: (i, j)),
        out_shape=jax.ShapeDtypeStruct((M, N), F32),
        compiler_params=_params(("parallel", "parallel")),
    )(a, w, *xs, modg)


def _gate_proj_kernel(a_ref, w_ref, p1_ref, p2_ref, o_ref, *, kind, H):
    y = jnp.dot(a_ref[...], w_ref[...], preferred_element_type=F32)
    col = lax.broadcasted_iota(jnp.int32, y.shape, 1)
    if kind == "gdn_gla":
        loga = -jnp.exp(p1_ref[...]) * _softplus(y + p2_ref[...])
        o_ref[...] = jnp.where(col < H, loga, jnp.where(col < 2 * H, _sigmoid(y), y))
    else:
        z = y + p1_ref[...]
        o_ref[...] = jnp.where(col < H, z, _log_sigmoid(z))


def _gate_proj(a, w, p1, p2, kind, H):
    M, K = a.shape
    N = w.shape[1]
    tm = _pick(M, 1024, CHUNK)
    return pl.pallas_call(
        functools.partial(_gate_proj_kernel, kind=kind, H=H),
        grid=(M // tm,),
        in_specs=[pl.BlockSpec((tm, K), lambda i: (i, 0)),
                  pl.BlockSpec((K, N), lambda i: (0, 0)),
                  pl.BlockSpec((1, N), lambda i: (0, 0)),
                  pl.BlockSpec((1, N), lambda i: (0, 0))],
        out_specs=pl.BlockSpec((tm, N), lambda i: (i, 0)),
        out_shape=jax.ShapeDtypeStruct((M, N), F32),
        compiler_params=_params(("parallel",)),
    )(a, w, p1, p2)


def _gdn_prep_kernel(x_ref, p_ref, cs_ref, cw_ref, gcol_ref, grow_ref,
                     wq_ref, u_ref, pk_out_ref, el_ref, nb_ref, in_scr, cv_scr, *, HB, dk, dv, sq):
    L = CHUNK
    g = pl.program_id(1)
    _, first, last, in_p = sq.pos(g)
    Wc = cw_ref.shape[0]
    npv = Wc - 1
    HK = HB * dk
    ncol = x_ref.shape[1]

    in_scr[8 - npv:8, :] = jnp.where(first, jnp.where(in_p, 0.0, cs_ref[0]), p_ref[8 - npv:8, :])
    in_scr[8:8 + L, :] = x_ref[...]
    cwid = _pick(ncol, 512, LANE)
    for c in range(0, ncol, cwid):
        cols = slice(c, c + cwid)
        y = in_scr[8 - npv:8 - npv + L, cols] * cw_ref[0:1, cols]
        for t in range(1, Wc):
            y = y + in_scr[8 - npv + t:8 - npv + t + L, cols] * cw_ref[t:t + 1, cols]
        cv_scr[:, cols] = y * _sigmoid(y)

    ii, jj = _tri_masks(L)
    incl = jj <= ii
    strict = jj < ii
    eye = (ii == jj).astype(F32)
    eye_b = eye.astype(BF16)
    gcol = gcol_ref[0]
    gc_all = _dot_hi(incl.astype(F32), gcol[:, :HB])
    gr_all = _dot_hi(grow_ref[0, 0], (ii <= jj).astype(F32))
    n_sq = int(np.log2(L)) - 1

    heads = range(HB)
    sks = [slice(j * dk, (j + 1) * dk) for j in heads]
    svs = [slice(j * dv, (j + 1) * dv) for j in heads]
    kn, gc, beta, eg, gL, kb, Ps, T = ([None] * HB for _ in range(8))
    for j in heads:
        q, k = cv_scr[:, sks[j]], cv_scr[:, HK + j * dk:HK + (j + 1) * dk]
        qn = (q * lax.rsqrt(jnp.sum(q * q, axis=-1, keepdims=True) + EPS)) * (dk ** -0.5)
        kn[j] = k * lax.rsqrt(jnp.sum(k * k, axis=-1, keepdims=True) + EPS)
        gc[j] = gc_all[:, j:j + 1]
        beta[j] = gcol[:, HB + j:HB + j + 1]
        dec_i = jnp.exp(jnp.where(incl, gc[j] - gr_all[j:j + 1, :], -jnp.inf))
        eg[j] = jnp.exp(gc[j])
        gL[j] = gc[j][L - 1:L, :]
        kb[j] = kn[j].astype(BF16)
        qb = qn.astype(BF16)
        wq_ref[0, L:2 * L, sks[j]] = (eg[j] * qn).astype(BF16)
        pk_out_ref[0, j, 0:L, :] = (dec_i * _dot_nt(qb, kb[j])).astype(BF16)
        A = beta[j] * jnp.where(strict, dec_i, 0.0) * _dot_nt(kb[j], kb[j])
        T[j] = eye - A
        Ps[j] = _split_bf16(-A)
    for _ in range(n_sq):
        for j in heads:
            Ps[j] = _split_bf16(_dot_x3(Ps[j], Ps[j]))
        for j in heads:
            T[j] = T[j] + _dot_x3(_split_bf16(T[j]), Ps[j])
    for j in heads:
        v = cv_scr[:, 2 * HK + j * dv:2 * HK + (j + 1) * dv]
        rhs = jnp.concatenate([(beta[j] * eg[j]) * kn[j], beta[j] * v], axis=1)
        WU = _dot(T[j], rhs)
        wq_ref[0, 0:L, sks[j]] = WU[:, :dk].astype(BF16)
        u_ref[:, svs[j]] = WU[:, dk:]
        kw = (kn[j] * jnp.exp(gL[j] - gc[j])).astype(BF16)
        pk_out_ref[0, j, L:L + dk, :] = _dot_tn(kw, eye_b).astype(BF16)
        el_ref[0, :, svs[j]] = jnp.broadcast_to(jnp.exp(gL[j]), (SUBLANE, dv))

    @pl.when(last)
    def _():
        nb_ref[0] = x_ref[L - npv:L, :]


def _gdn_prep(proj, gates_col, gates_row, conv_state, conv_w, H, dk, dv, sq):
    assert dk == dv
    HK = H * dk
    W3 = 3 * HK
    Bt = sq.Bp + sq.Bs
    Wc = conv_w.shape[0]
    npv = Wc - 1
    in_specs = [pl.BlockSpec((CHUNK, W3), lambda h, g: (g, 0)),
                pl.BlockSpec((SUBLANE, W3), lambda h, g: (jnp.maximum(g * (CHUNK // SUBLANE) - 1, 0), 0)),
                pl.BlockSpec((1, npv, W3), lambda h, g: (sq.seq_s(g), 0, 0)),
                pl.BlockSpec((Wc, W3), lambda h, g: (0, 0)),
                pl.BlockSpec((1, CHUNK, 2 * H), lambda h, g: (0, g, 0)),
                pl.BlockSpec((1, 1, H, CHUNK), lambda h, g: (g, 0, 0, 0))]
    out_specs = [pl.BlockSpec((1, 2 * CHUNK, HK), lambda h, g: (g, 0, 0)),
                 pl.BlockSpec((CHUNK, HK), lambda h, g: (g, 0)),
                 pl.BlockSpec((1, H, CHUNK + dk, CHUNK), lambda h, g: (g, 0, 0, 0)),
                 pl.BlockSpec((1, SUBLANE, HK), lambda h, g: (g, 0, 0)),
                 pl.BlockSpec((1, npv, W3), lambda h, g: (sq.pos(g)[0], 0, 0))]
    out_shape = [jax.ShapeDtypeStruct((sq.NG, 2 * CHUNK, HK), BF16),
                 jax.ShapeDtypeStruct((sq.M, HK), F32),
                 jax.ShapeDtypeStruct((sq.NG, H, CHUNK + dk, CHUNK), BF16),
                 jax.ShapeDtypeStruct((sq.NG, SUBLANE, HK), F32),
                 jax.ShapeDtypeStruct((Bt, npv, W3), F32)]
    return pl.pallas_call(
        functools.partial(_gdn_prep_kernel, HB=H, dk=dk, dv=dv, sq=sq),
        grid=(1, sq.NG),
        in_specs=in_specs,
        out_specs=out_specs,
        out_shape=out_shape,
        scratch_shapes=[pltpu.VMEM((SUBLANE + CHUNK, W3), F32), pltpu.VMEM((CHUNK, W3), F32)],
        compiler_params=_params(("parallel", "arbitrary")),
    )(proj, proj, conv_state, conv_w, gates_col, gates_row)


def _gdn_scan_kernel(wq_ref, u_ref, pk_ref, el_ref, ga_ref, s0_ref, ng_ref, buf_ref,
                     o_ref, sp_ref, ss_ref, s_scr, *, H, dk, dv, sq):
    del buf_ref
    L = CHUNK
    g = pl.program_id(0)
    _, first, last, in_p = sq.pos(g)

    @pl.when(first & in_p)
    def _():
        s_scr[...] = jnp.zeros_like(s_scr)

    @pl.when(first & jnp.logical_not(in_p))
    def _():
        s_scr[...] = s0_ref[0]

    heads = range(H)
    sks = [slice(j * dk, (j + 1) * dk) for j in heads]
    svs = [slice(j * dv, (j + 1) * dv) for j in heads]
    M1, M2 = [None] * H, [None] * H
    for j in heads:
        M1[j] = jnp.dot(wq_ref[0, :, sks[j]], s_scr[j].astype(BF16), preferred_element_type=F32)
    for j in heads:
        E = u_ref[:, svs[j]] - M1[j][:L]
        M2[j] = jnp.dot(pk_ref[0, j], E.astype(BF16), preferred_element_type=F32)
    for j in heads:
        s_scr[j] = el_ref[0, 0:1, svs[j]] * s_scr[j] + M2[j][L:]
        o = M1[j][L:] + M2[j][:L]
        on = (o * lax.rsqrt(jnp.mean(o * o, axis=-1, keepdims=True) + EPS)) * ng_ref[...]
        ga = ga_ref[:, svs[j]]
        o_ref[:, svs[j]] = (on * (ga * _sigmoid(ga))).astype(o_ref.dtype)

    @pl.when(last & in_p)
    def _():
        sp_ref[0] = s_scr[...]

    @pl.when(last & jnp.logical_not(in_p))
    def _():
        ss_ref[0] = s_scr[...]


def _gdn_scan(wq, u, pk, el, proj, gate_off, S0, norm_g, buf, H, dk, dv, sq):
    HK, HV = H * dk, H * dv
    in_specs = [pl.BlockSpec((1, 2 * CHUNK, HK), lambda g: (g, 0, 0)),
                pl.BlockSpec((CHUNK, HV), lambda g: (g, 0)),
                pl.BlockSpec((1, H, CHUNK + dk, CHUNK), lambda g: (g, 0, 0, 0)),
                pl.BlockSpec((1, SUBLANE, HV), lambda g: (g, 0, 0)),
                pl.BlockSpec((CHUNK, HV), lambda g: (g, gate_off // HV)),
                pl.BlockSpec((1, H, dk, dv), lambda g: (sq.seq_s(g), 0, 0, 0)),
                pl.BlockSpec((1, dv), lambda g: (0, 0)),
                pl.BlockSpec(memory_space=pl.ANY)]
    out_specs = [pl.BlockSpec((CHUNK, HV), lambda g: (g, 0)),
                 pl.BlockSpec((1, H, dk, dv), lambda g: (sq.seq_p(g), 0, 0, 0)),
                 pl.BlockSpec((1, H, dk, dv), lambda g: (sq.seq_s(g), 0, 0, 0))]
    out_shape = [jax.ShapeDtypeStruct(buf.shape, buf.dtype),
                 jax.ShapeDtypeStruct((sq.Bp, H, dk, dv), F32),
                 jax.ShapeDtypeStruct((sq.Bs, H, dk, dv), F32)]
    return pl.pallas_call(
        functools.partial(_gdn_scan_kernel, H=H, dk=dk, dv=dv, sq=sq),
        grid=(sq.NG,),
        in_specs=in_specs,
        out_specs=out_specs,
        out_shape=out_shape,
        scratch_shapes=[pltpu.VMEM((H, dk, dv), F32)],
        input_output_aliases={7: 0},
        compiler_params=_params(("arbitrary",)),
    )(wq, u, pk, el, proj, S0, norm_g.reshape(1, dv), buf)


def _gla_kernel(x_ref, lr_ref, w2_ref, gb_ref, s0_ref, ng_ref, buf_ref,
                o_ref, sp_ref, ss_ref, s_scr, *, HB, dk, dv, sq):
    del buf_ref
    HK, HV = HB * dk, HB * dv
    q_ref, k_ref = x_ref.at[:, 0:HK], x_ref.at[:, HK:2 * HK]
    v_ref, r_ref = x_ref.at[:, 2 * HK:2 * HK + HV], x_ref.at[:, 2 * HK + HV:2 * HK + 2 * HV]
    L = CHUNK
    g = pl.program_id(1)
    _, first, last, in_p = sq.pos(g)

    @pl.when(first & in_p)
    def _():
        s_scr[...] = jnp.zeros_like(s_scr)

    @pl.when(first & jnp.logical_not(in_p))
    def _():
        s_scr[...] = s0_ref[0]

    ii, jj = _tri_masks(L)
    incl = jj <= ii
    tri = incl.astype(F32)
    krow = lax.broadcasted_iota(jnp.int32, (L, dk), 0)
    di = lax.broadcasted_iota(jnp.int32, (dk, dk), 0)
    dj = lax.broadcasted_iota(jnp.int32, (dk, dk), 1)
    lr = lr_ref[...]

    heads = range(HB)
    sks = [slice(j * dk, (j + 1) * dk) for j in heads]
    svs = [slice(j * dv, (j + 1) * dv) for j in heads]
    G, att = [None] * HB, [None] * HB
    for j in heads:
        x = _dot(lr, w2_ref[:, sks[j]]) + gb_ref[:, sks[j]]
        G[j] = _dot_hi(tri, _log_sigmoid(x) / GLA_TAU)
    for j in heads:
        q = q_ref[:, sks[j]] * (dk ** -0.5)
        k = k_ref[:, sks[j]]
        rows = []
        for b in range(L // GLA_SUB):
            lo, hi = b * GLA_SUB, (b + 1) * GLA_SUB
            mid = lo + GLA_SUB // 2
            ref_g = G[j][mid:mid + 1, :]
            qg = q[lo:hi, :] * jnp.exp(G[j][lo:hi, :] - ref_g)
            kg = k * jnp.exp(jnp.where(krow < hi, ref_g - G[j], -jnp.inf))
            rows.append(_dot_nt(qg, kg))
        att[j] = jnp.where(incl, jnp.concatenate(rows, axis=0), 0.0)
    for j in heads:
        q = q_ref[:, sks[j]] * (dk ** -0.5)
        o = _dot(q * jnp.exp(G[j]), s_scr[j]) + _dot(att[j], v_ref[:, svs[j]])
        on = (o * lax.rsqrt(jnp.mean(o * o, axis=-1, keepdims=True) + EPS)) * ng_ref[...]
        r = r_ref[:, svs[j]]
        o_ref[:, svs[j]] = (on * (r * _sigmoid(r))).astype(o_ref.dtype)
    for j in heads:
        GL = G[j][L - 1:L, :]
        kS = k_ref[:, sks[j]] * jnp.exp(GL - G[j])
        a_col = jnp.sum(jnp.where(di == dj, jnp.broadcast_to(jnp.exp(GL), (dk, dk)), 0.0),
                        axis=1, keepdims=True)
        s_scr[j] = a_col * s_scr[j] + _dot_tn(kS, v_ref[:, svs[j]])

    @pl.when(last & in_p)
    def _():
        sp_ref[0] = s_scr[...]

    @pl.when(last & jnp.logical_not(in_p))
    def _():
        ss_ref[0] = s_scr[...]


def _gla(proj, lr, w2, gb, S0, norm_g, buf, buf_col_off, H, dk, dv, sq):
    HG, HB = 1, H
    WK, WV = HB * dk, HB * dv
    R = lr.shape[1]
    assert proj.shape[1] == 2 * WK + 2 * WV and buf_col_off % WV == 0

    in_specs = [pl.BlockSpec((CHUNK, 2 * WK + 2 * WV), lambda h, g: (g, 0)),
                pl.BlockSpec((CHUNK, R), lambda h, g: (g, 0)),
                pl.BlockSpec((R, WK), lambda h, g: (0, h)),
                pl.BlockSpec((1, WK), lambda h, g: (0, h)),
                pl.BlockSpec((1, HB, dk, dv), lambda h, g: (sq.seq_s(g), h, 0, 0)),
                pl.BlockSpec((1, dv), lambda h, g: (0, 0)),
                pl.BlockSpec(memory_space=pl.ANY)]
    out_specs = [pl.BlockSpec((CHUNK, WV), lambda h, g: (g, buf_col_off // WV + h)),
                 pl.BlockSpec((1, HB, dk, dv), lambda h, g: (sq.seq_p(g), h, 0, 0)),
                 pl.BlockSpec((1, HB, dk, dv), lambda h, g: (sq.seq_s(g), h, 0, 0))]
    out_shape = [jax.ShapeDtypeStruct(buf.shape, buf.dtype),
                 jax.ShapeDtypeStruct((sq.Bp, H, dk, dv), F32),
                 jax.ShapeDtypeStruct((sq.Bs, H, dk, dv), F32)]
    return pl.pallas_call(
        functools.partial(_gla_kernel, HB=HB, dk=dk, dv=dv, sq=sq),
        grid=(HG, sq.NG),
        in_specs=in_specs,
        out_specs=out_specs,
        out_shape=out_shape,
        scratch_shapes=[pltpu.VMEM((HB, dk, dv), F32)],
        input_output_aliases={6: 0},
        compiler_params=_params(("parallel", "arbitrary")),
    )(proj, lr, w2, gb.reshape(1, -1), S0, norm_g.reshape(1, dv), buf)


def _mlstm_kernel(x_ref, gcol_ref, grow_ref, c0_ref, n0_ref, m0_ref, ng_ref,
                  o_ref, cp_ref, cs_ref, np_ref, ns_ref, mp_ref, ms_ref, c_scr, n_scr, m_scr,
                  *, HB, dk, dv, sq):
    HK, HV = HB * dk, HB * dv
    q_ref, k_ref = x_ref.at[:, 0:HK], x_ref.at[:, HK:2 * HK]
    v_ref, og_ref = x_ref.at[:, 2 * HK:2 * HK + HV], x_ref.at[:, 2 * HK + HV:2 * HK + 2 * HV]
    L = CHUNK
    g = pl.program_id(1)
    _, first, last, in_p = sq.pos(g)

    @pl.when(first & in_p)
    def _():
        c_scr[...] = jnp.zeros_like(c_scr)
        n_scr[...] = jnp.zeros_like(n_scr)
        m_scr[...] = jnp.zeros_like(m_scr)

    @pl.when(first & jnp.logical_not(in_p))
    def _():
        c_scr[...] = c0_ref[0]
        n_scr[...] = n0_ref[0, 0]
        m_scr[...] = m0_ref[0, 0]

    ii, jj = _tri_masks(L)
    incl = jj <= ii
    gcol = gcol_ref[0]
    grow = grow_ref[0, 0]
    bc_all = _dot_hi(incl.astype(F32), gcol[:, HB:])
    br_all = _dot_hi(grow[HB:, :], (ii <= jj).astype(F32))

    heads = range(HB)
    sks = [slice(j * dk, (j + 1) * dk) for j in heads]
    svs = [slice(j * dv, (j + 1) * dv) for j in heads]
    mi, w_inter, expD, wk, decay, qb, Wm = ([None] * HB for _ in range(7))
    for j in heads:
        bc = bc_all[:, j:j + 1]
        m_prev = m_scr[j:j + 1, :]
        Dlog = jnp.where(incl, bc - br_all[j:j + 1, :] + grow[j:j + 1, :], -jnp.inf)
        inter = bc + m_prev
        mi[j] = jnp.maximum(inter, jnp.max(Dlog, axis=-1, keepdims=True))
        w_inter[j] = jnp.exp(inter - mi[j])
        expD[j] = jnp.exp(Dlog - mi[j])
        mL = mi[j][L - 1:L, :]
        bL = bc[L - 1:L, :]
        wk[j] = jnp.exp(bL - bc + gcol[:, j:j + 1] - mL)
        decay[j] = jnp.exp(bL + m_prev - mL)
        m_scr[j:j + 1, :] = mL
    for j in heads:
        qb[j] = (q_ref[:, sks[j]] * (dk ** -0.5)).astype(BF16)
        Wm[j] = expD[j] * _dot_nt(qb[j], k_ref[:, sks[j]])
    for j in heads:
        q = q_ref[:, sks[j]] * (dk ** -0.5)
        num = w_inter[j] * _dot(qb[j], c_scr[j]) + _dot(Wm[j], v_ref[:, svs[j]])
        den = (w_inter[j] * jnp.sum(q * n_scr[j:j + 1, :], axis=-1, keepdims=True)
               + jnp.sum(Wm[j], axis=-1, keepdims=True))
        h = num / jnp.maximum(jnp.abs(den), jnp.exp(-mi[j]))
        hn = (h * lax.rsqrt(jnp.mean(h * h, axis=-1, keepdims=True) + EPS)) * ng_ref[...]
        o_ref[:, svs[j]] = (hn * _sigmoid(og_ref[:, svs[j]])).astype(o_ref.dtype)
    for j in heads:
        kw = k_ref[:, sks[j]] * wk[j]
        c_scr[j] = decay[j] * c_scr[j] + _dot_tn(kw, v_ref[:, svs[j]])
        n_scr[j:j + 1, :] = decay[j] * n_scr[j:j + 1, :] + jnp.sum(kw, axis=0, keepdims=True)

    @pl.when(last & in_p)
    def _():
        cp_ref[0] = c_scr[...]
        np_ref[0, 0] = n_scr[...]
        mp_ref[0, 0] = m_scr[...]

    @pl.when(last & jnp.logical_not(in_p))
    def _():
        cs_ref[0] = c_scr[...]
        ns_ref[0, 0] = n_scr[...]
        ms_ref[0, 0] = m_scr[...]


def _mlstm(proj, gates_col, gates_row, C0, n0, m0, norm_g, H, dk, dv, sq):
    HG, HB = 1, H
    WK, WV = HB * dk, HB * dv
    assert proj.shape[1] == 2 * WK + 2 * WV

    def st(shape, seq_of):
        return pl.BlockSpec((1,) + shape, lambda h, g: (seq_of(g), h) + (0,) * (len(shape) - 1))

    in_specs = [pl.BlockSpec((CHUNK, 2 * WK + 2 * WV), lambda h, g: (g, 0)),
                pl.BlockSpec((1, CHUNK, 2 * HB), lambda h, g: (h, g, 0)),
                pl.BlockSpec((1, 1, 2 * HB, CHUNK), lambda h, g: (g, h, 0, 0)),
                st((HB, dk, dv), sq.seq_s), st((1, HB, dk), sq.seq_s), st((1, HB, 1), sq.seq_s),
                pl.BlockSpec((1, dv), lambda h, g: (0, 0))]
    out_specs = [pl.BlockSpec((CHUNK, WV), lambda h, g: (g, h)),
                 st((HB, dk, dv), sq.seq_p), st((HB, dk, dv), sq.seq_s),
                 st((1, HB, dk), sq.seq_p), st((1, HB, dk), sq.seq_s),
                 st((1, HB, 1), sq.seq_p), st((1, HB, 1), sq.seq_s)]
    out_shape = [jax.ShapeDtypeStruct((sq.M, H * dv), BF16),
                 jax.ShapeDtypeStruct((sq.Bp, H, dk, dv), F32),
                 jax.ShapeDtypeStruct((sq.Bs, H, dk, dv), F32),
                 jax.ShapeDtypeStruct((sq.Bp, HG, HB, dk), F32),
                 jax.ShapeDtypeStruct((sq.Bs, HG, HB, dk), F32),
                 jax.ShapeDtypeStruct((sq.Bp, HG, HB, 1), F32),
                 jax.ShapeDtypeStruct((sq.Bs, HG, HB, 1), F32)]
    return pl.pallas_call(
        functools.partial(_mlstm_kernel, HB=HB, dk=dk, dv=dv, sq=sq),
        grid=(HG, sq.NG),
        in_specs=in_specs,
        out_specs=out_specs,
        out_shape=out_shape,
        scratch_shapes=[pltpu.VMEM((HB, dk, dv), F32),
                        pltpu.VMEM((HB, dk), F32),
                        pltpu.VMEM((HB, 1), F32)],
        compiler_params=_params(("parallel", "arbitrary")),
    )(proj, gates_col, gates_row, C0,
      n0.reshape(sq.Bs, HG, HB, dk), m0.reshape(sq.Bs, HG, HB, 1), norm_g.reshape(1, dv))


def _gates_col(g, n_kinds, HG, HB):
    M = g.shape[0]
    return g.reshape(M, n_kinds, HG, HB).transpose(2, 0, 1, 3).reshape(HG, M, n_kinds * HB)


def _gates_row(g, n_kinds, HG, HB):
    M = g.shape[0]
    g = g.reshape(M // CHUNK, CHUNK, n_kinds, HG, HB).transpose(0, 3, 2, 4, 1)
    return g.reshape(M // CHUNK, HG, n_kinds * HB, CHUNK)


def kernel(x_prompt, x_sample, state_gdn_S, state_gdn_conv, state_gla_S, state_mlstm_C, state_mlstm_n,
           state_mlstm_m, c_prompt, c_sample, ada_w, ada_b, norm_g, ffn_w_in, ffn_w_out, gdn_gla_w_in,
           gdn_gla_w_out, gdn_conv_w, gdn_A_log, gdn_dt_bias, gdn_norm_g, gla_gate_w2, gla_gate_b,
           gla_norm_g, mlstm_w_in, mlstm_w_out, mlstm_gate_b, mlstm_norm_g, final_norm_g):
    Bp, Tp, D = x_prompt.shape
    Bs, Ts, _ = x_sample.shape
    depth = ada_w.shape[0]
    assert Tp % CHUNK == 0 and Ts % CHUNK == 0 and norm_g.shape[1] == 3
    sq = _Seqs(Bp, Tp // CHUNK, Bs, Ts // CHUNK)
    Mp, Ms = sq.Mp, sq.Ms

    _, _, Hg, dkg, dvg = state_gdn_S.shape
    _, _, Hl, dkl, dvl = state_gla_S.shape
    _, _, Hm, dkm, dvm = state_mlstm_C.shape
    Wg, Wl, Wm = Hg * dvg, Hl * dvl, Hm * dvm
    R = gla_gate_w2.shape[1]
    npv = state_gdn_conv.shape[2]
    assert npv + 1 == gdn_conv_w.shape[1] and npv <= SUBLANE and npv <= CHUNK

    ffn_w_out_b = ffn_w_out.astype(BF16)
    gdn_gla_w_in_b = gdn_gla_w_in.astype(BF16)
    gdn_gla_w_out_b = gdn_gla_w_out.astype(BF16)
    mlstm_w_in_b = mlstm_w_in.astype(BF16)
    mlstm_w_out_b = mlstm_w_out.astype(BF16)

    modg = _ada_mod(jnp.concatenate([c_prompt, c_sample], axis=0), ada_w, ada_b, sq)

    def ffn(x, layer, f, s):
        h = _prenorm(x, norm_g[layer, s], modg, layer, s, sq)
        hh = _matmul_swiglu(h, ffn_w_in, (layer, f))
        return _matmul_residual(hh, ffn_w_out_b, (layer, f), x, modg, layer, s, 0.5, sq, 512)

    x = (x_prompt.reshape(Mp, D), x_sample.reshape(Ms, D))
    new_gS, new_conv, new_lS, new_C, new_n, new_m = [], [], [], [], [], []
    for layer in range(depth):
        x = ffn(x, layer, 0, 0)
        i = layer // 2
        h = _prenorm(x, norm_g[layer, 1], modg, layer, 1, sq)
        if layer % 2 == 0:
            nqk = Hg * dkg
            nA = 2 * nqk + 2 * Wg
            nB = 2 * Hl * dkl + 2 * Wl
            oB = nA + 2 * Hg
            wb = gdn_gla_w_in_b[i]
            w_gate = jnp.concatenate([wb[:, nA:oB], wb[:, oB + nB:]], axis=1)
            ngc = w_gate.shape[1]
            p1 = jnp.zeros((1, ngc), F32).at[0, :Hg].set(gdn_A_log[i].astype(F32))
            p2 = jnp.zeros((1, ngc), F32).at[0, :Hg].set(gdn_dt_bias[i].astype(F32))
            projA = _matmul(h, gdn_gla_w_in_b, (i,), nA, F32)
            projB = _matmul(h, wb[:, oB:oB + nB], (), nB, F32)
            gates = _gate_proj(h, w_gate, p1, p2, "gdn_gla", Hg)
            gcol = _gates_col(gates[:, :2 * Hg], 2, 1, Hg)
            grow = _gates_row(gates[:, :Hg], 1, 1, Hg)
            wq, u, pk, el, nb = _gdn_prep(
                projA, gcol, grow, state_gdn_conv[i].astype(F32), gdn_conv_w[i].astype(F32),
                Hg, dkg, dvg, sq)
            buf = jnp.zeros((sq.M, Wg + Wl), BF16)
            buf, gS_p, gS_s = _gdn_scan(wq, u, pk, el, projA, 2 * nqk + Wg, state_gdn_S[i].astype(F32),
                                        gdn_norm_g[i].astype(F32), buf, Hg, dkg, dvg, sq)
            buf, lS_p, lS_s = _gla(
                projB, gates[:, 2 * Hg:], gla_gate_w2[i].astype(F32), gla_gate_b[i].astype(F32),
                state_gla_S[i].astype(F32), gla_norm_g[i].astype(F32), buf, Wg, Hl, dkl, dvl, sq)
            new_gS.append((gS_p, gS_s))
            new_conv.append((nb[:Bp], nb[Bp:]))
            new_lS.append((lS_p, lS_s))
            w_out, lead = gdn_gla_w_out_b, (i,)
        else:
            nqk = Hm * dkm
            nmain = 2 * nqk + 2 * Wm
            gb = mlstm_gate_b[i].astype(F32).reshape(1, 2 * Hm)
            proj = _matmul(h, mlstm_w_in_b, (i,), nmain, F32)
            gates = _gate_proj(h, mlstm_w_in_b[i][:, nmain:], gb, gb, "mlstm", Hm)
            gcol = _gates_col(gates, 2, 1, Hm)
            grow = _gates_row(gates, 2, 1, Hm)
            buf, C_p, C_s, n_p, n_s, m_p, m_s = _mlstm(
                proj, gcol, grow, state_mlstm_C[i].astype(F32), state_mlstm_n[i].astype(F32),
                state_mlstm_m[i].astype(F32), mlstm_norm_g[i].astype(F32), Hm, dkm, dvm, sq)
            new_C.append((C_p, C_s))
            new_n.append((n_p.reshape(Bp, Hm, dkm), n_s.reshape(Bs, Hm, dkm)))
            new_m.append((m_p.reshape(Bp, Hm), m_s.reshape(Bs, Hm)))
            w_out, lead = mlstm_w_out_b, (i,)
        x = _matmul_residual(buf, w_out, lead, x, modg, layer, 1, 1.0, sq, 1024)
        x = ffn(x, layer, 1, 2)

    y_p = _final_norm(x, final_norm_g, 0, Mp).reshape(Bp, Tp, D)
    y_s = _final_norm(x, final_norm_g, Mp, Ms).reshape(Bs, Ts, D)

    def pair(states):
        return jnp.stack([p for p, _ in states]), jnp.stack([s for _, s in states])

    gS_p, gS_s = pair(new_gS)
    gc_p, gc_s = pair(new_conv)
    lS_p, lS_s = pair(new_lS)
    C_p, C_s = pair(new_C)
    n_p, n_s = pair(new_n)
    m_p, m_s = pair(new_m)
    return (y_p, y_s, gS_p, gS_s, gc_p, gc_s, lS_p, lS_s, C_p, C_s, n_p, n_s, m_p, m_s)
```

```python
import functools

import numpy as np
import jax
import jax.numpy as jnp
from jax import lax
from jax.experimental import pallas as pl
from jax.experimental.pallas import tpu as pltpu

F32 = jnp.float32
BF16 = jnp.bfloat16
CHUNK = 64
EPS = 1e-6
GLA_TAU = 16.0
GLA_SUB = 16
V7X_VMEM_LIMIT = 58 * 1024 * 1024
LANE = 128
SUBLANE = 8
HI = lax.Precision.HIGHEST


def _pick(dim, target, align):
    best = None
    t = align
    while t <= min(dim, target):
        if dim % t == 0:
            best = t
        t += align
    return dim if best is None else best


def _sigmoid(x):
    return 1.0 / (1.0 + jnp.exp(-x))


def _log_sigmoid(x):
    return jnp.minimum(x, 0.0) - jnp.log1p(jnp.exp(-jnp.abs(x)))


def _softplus(x):
    return jnp.maximum(x, 0.0) + jnp.log1p(jnp.exp(-jnp.abs(x)))


def _dot(a, b):
    return jnp.dot(a.astype(BF16), b.astype(BF16), preferred_element_type=F32)


def _dot_nt(a, b):
    return lax.dot_general(a.astype(BF16), b.astype(BF16), (((1,), (1,)), ((), ())),
                           preferred_element_type=F32)


def _dot_tn(a, b):
    return lax.dot_general(a.astype(BF16), b.astype(BF16), (((0,), (0,)), ((), ())),
                           preferred_element_type=F32)


def _dot_hi(a, b):
    return jnp.dot(a, b, precision=HI, preferred_element_type=F32)


def _split_bf16(a):
    hi = a.astype(BF16)
    return hi, (a - hi.astype(F32)).astype(BF16)


def _dot_x3(a, b):
    (ah, al), (bh, bl) = a, b
    m = ah.shape[0]
    d = functools.partial(jnp.dot, preferred_element_type=F32)
    s = d(jnp.concatenate([ah, al], axis=0), bh)
    return s[:m] + (d(ah, bl) + s[m:])


def _params(sem):
    return pltpu.CompilerParams(dimension_semantics=sem, vmem_limit_bytes=V7X_VMEM_LIMIT)


def _chunk_pos(g, Bp, NCp, NCs):
    npc = Bp * NCp
    in_p = g < npc
    r = g - npc
    seq = jnp.where(in_p, g // NCp, Bp + r // NCs)
    cin = jnp.where(in_p, g % NCp, r % NCs)
    last = jnp.where(in_p, NCp - 1, NCs - 1)
    return seq, cin == 0, cin == last, in_p


def _tri_masks(L):
    ii = lax.broadcasted_iota(jnp.int32, (L, L), 0)
    jj = lax.broadcasted_iota(jnp.int32, (L, L), 1)
    return ii, jj


class _Seqs:
    def __init__(self, Bp, NCp, Bs, NCs):
        self.Bp, self.NCp, self.Bs, self.NCs = Bp, NCp, Bs, NCs
        self.Mp, self.Ms = Bp * NCp * CHUNK, Bs * NCs * CHUNK
        self.M = self.Mp + self.Ms
        self.NG = self.M // CHUNK
        self.Mg = int(np.gcd(self.Mp, self.Ms))

    def pos(self, g):
        return _chunk_pos(g, self.Bp, self.NCp, self.NCs)

    def seq_p(self, g):
        return jnp.minimum(self.pos(g)[0], self.Bp - 1)

    def seq_s(self, g):
        return jnp.maximum(self.pos(g)[0] - self.Bp, 0)


def _x_specs(x, tm, tn, sq, col_of):
    if isinstance(x, tuple):
        npt = sq.Mp // tm
        return [pl.BlockSpec((tm, tn), lambda i, *r: (jnp.minimum(i, npt - 1), col_of(*r))),
                pl.BlockSpec((tm, tn), lambda i, *r: (jnp.maximum(i - npt, 0), col_of(*r)))], list(x), npt
    return [pl.BlockSpec((tm, tn), lambda i, *r: (i, col_of(*r)))], [x], 0


def _on_source(x_refs, npt, body):
    if len(x_refs) == 1:
        body(x_refs[0])
    else:
        i = pl.program_id(0)
        pl.when(i < npt)(lambda: body(x_refs[0]))
        pl.when(i >= npt)(lambda: body(x_refs[1]))


def _ada_kernel(c_ref, w_ref, b_ref, o_ref, res_scr, *, sq):
    c = c_ref[...]
    sc = (c * _sigmoid(c)).astype(BF16)
    res_scr[...] = jnp.dot(sc, w_ref[0].astype(BF16), preferred_element_type=F32) + b_ref[0]
    tn = res_scr.shape[1]
    for b in range(sq.Bp):
        o_ref[0, b * sq.NCp:(b + 1) * sq.NCp, :] = jnp.broadcast_to(res_scr[b:b + 1, :], (sq.NCp, tn))
    base = sq.Bp * sq.NCp
    if sq.NCs == 1:
        o_ref[0, base:base + sq.Bs, :] = res_scr[sq.Bp:sq.Bp + sq.Bs, :]
    else:
        for b in range(sq.Bs):
            o_ref[0, base + b * sq.NCs:base + (b + 1) * sq.NCs, :] = jnp.broadcast_to(
                res_scr[sq.Bp + b:sq.Bp + b + 1, :], (sq.NCs, tn))


def _ada_mod(c, ada_w, ada_b, sq):
    depth, D, N = ada_w.shape
    Bt = c.shape[0]
    tn = _pick(N, 1024, LANE)
    return pl.pallas_call(
        functools.partial(_ada_kernel, sq=sq),
        grid=(depth, N // tn),
        in_specs=[pl.BlockSpec((Bt, D), lambda l, j: (0, 0)),
                  pl.BlockSpec((1, D, tn), lambda l, j: (l, 0, j)),
                  pl.BlockSpec((1, 1, tn), lambda l, j: (l, 0, j))],
        out_specs=pl.BlockSpec((1, sq.NG, tn), lambda l, j: (l, 0, j)),
        out_shape=jax.ShapeDtypeStruct((depth, sq.NG, N), F32),
        scratch_shapes=[pltpu.VMEM((Bt, tn), F32)],
        compiler_params=_params(("parallel", "parallel")),
    )(c, ada_w, ada_b.reshape(depth, 1, N))


def _row_rms_scale(x_ref, rows):
    D = x_ref.shape[1]
    acc = None
    for c in range(0, D, LANE):
        xc = x_ref[rows, c:c + LANE]
        acc = xc * xc if acc is None else acc + xc * xc
    return lax.rsqrt(jnp.sum(acc, axis=-1, keepdims=True) * (1.0 / D) + EPS)


def _prenorm_kernel(*refs, npt):
    *x_refs, g_ref, sc_ref, sh_ref, o_ref = refs

    def body(x_ref):
        D = x_ref.shape[1]
        cw = _pick(D, 512, LANE)
        for r in range(x_ref.shape[0] // CHUNK):
            rows = slice(r * CHUNK, (r + 1) * CHUNK)
            rinv = _row_rms_scale(x_ref, rows)
            for c in range(0, D, cw):
                cols = slice(c, c + cw)
                gm = g_ref[:, cols] * (1.0 + sc_ref[r:r + 1, cols])
                o_ref[rows, cols] = ((x_ref[rows, cols] * rinv) * gm + sh_ref[r:r + 1, cols]).astype(o_ref.dtype)

    _on_source(x_refs, npt, body)


def _prenorm(x, g, modg, layer, s, sq):
    D = g.shape[0]
    tr = _pick(sq.Mg, 512, CHUNK * SUBLANE)
    ng = tr // CHUNK
    x_specs, xs, npt = _x_specs(x, tr, D, sq, lambda: 0)
    return pl.pallas_call(
        functools.partial(_prenorm_kernel, npt=npt),
        grid=(sq.M // tr,),
        in_specs=x_specs + [pl.BlockSpec((1, D), lambda i: (0, 0)),
                            pl.BlockSpec((None, ng, D), lambda i: (layer, i, s * 3 + 1)),
                            pl.BlockSpec((None, ng, D), lambda i: (layer, i, s * 3))],
        out_specs=pl.BlockSpec((tr, D), lambda i: (i, 0)),
        out_shape=jax.ShapeDtypeStruct((sq.M, D), BF16),
        compiler_params=_params(("parallel",)),
    )(*xs, g.reshape(1, D), modg, modg)


def _final_norm_kernel(x_ref, g_ref, o_ref):
    D = x_ref.shape[1]
    cw = _pick(D, 512, LANE)
    rt = _pick(x_ref.shape[0], CHUNK, SUBLANE)
    for r in range(0, x_ref.shape[0], rt):
        rows = slice(r, r + rt)
        rinv = _row_rms_scale(x_ref, rows)
        for c in range(0, D, cw):
            cols = slice(c, c + cw)
            o_ref[rows, cols] = (x_ref[rows, cols] * rinv) * g_ref[:, cols]


def _final_norm(x, g, row0, nrows):
    D = x.shape[1]
    tr = _pick(int(np.gcd(nrows, row0)) if row0 else nrows, 256, SUBLANE)
    assert row0 % tr == 0 and nrows % tr == 0
    off = row0 // tr
    return pl.pallas_call(
        _final_norm_kernel,
        grid=(nrows // tr,),
        in_specs=[pl.BlockSpec((tr, D), lambda i: (i + off, 0)),
                  pl.BlockSpec((1, D), lambda i: (0, 0))],
        out_specs=pl.BlockSpec((tr, D), lambda i: (i, 0)),
        out_shape=jax.ShapeDtypeStruct((nrows, D), F32),
        compiler_params=_params(("parallel",)),
    )(x, g.reshape(1, D))


def _w_spec(w, lead, K, tn, col_of):
    return pl.BlockSpec((None,) * len(lead) + (K, tn), lambda i, j: tuple(lead) + (0, col_of(j)))


def _mm_kernel(a_ref, w_ref, o_ref):
    o_ref[...] = jnp.dot(a_ref[...], w_ref[...], preferred_element_type=F32).astype(o_ref.dtype)


def _matmul(a, w, lead, ncols, out_dtype):
    M, K = a.shape
    tm = _pick(M, 1024, CHUNK)
    tn = _pick(ncols, 1024, LANE)
    return pl.pallas_call(
        _mm_kernel,
        grid=(M // tm, ncols // tn),
        in_specs=[pl.BlockSpec((tm, K), lambda i, j: (i, 0)),
                  _w_spec(w, lead, K, tn, lambda j: j)],
        out_specs=pl.BlockSpec((tm, tn), lambda i, j: (i, j)),
        out_shape=jax.ShapeDtypeStruct((M, ncols), out_dtype),
        compiler_params=_params(("parallel", "parallel")),
    )(a, w)


def _mm_swiglu_kernel(a_ref, wg_ref, wu_ref, o_ref):
    wg = wg_ref[...].astype(BF16)
    wu = wu_ref[...].astype(BF16)
    rt = _pick(a_ref.shape[0], 768, CHUNK)
    for r in range(0, a_ref.shape[0], rt):
        a = a_ref[r:r + rt, :]
        g = jnp.dot(a, wg, preferred_element_type=F32)
        u = jnp.dot(a, wu, preferred_element_type=F32)
        o_ref[r:r + rt, :] = ((g * _sigmoid(g)) * u).astype(o_ref.dtype)


def _matmul_swiglu(a, w, lead):
    M, K = a.shape
    F = w.shape[-1] // 2
    tm = _pick(M, 3072, CHUNK)
    tn = _pick(F, 256, LANE)
    nf = F // tn
    return pl.pallas_call(
        _mm_swiglu_kernel,
        grid=(M // tm, nf),
        in_specs=[pl.BlockSpec((tm, K), lambda i, j: (i, 0), pipeline_mode=pl.Buffered(1)),
                  _w_spec(w, lead, K, tn, lambda j: j),
                  _w_spec(w, lead, K, tn, lambda j: nf + j)],
        out_specs=pl.BlockSpec((tm, tn), lambda i, j: (i, j)),
        out_shape=jax.ShapeDtypeStruct((M, F), BF16),
        compiler_params=_params(("parallel", "parallel")),
    )(a, w, w)


def _mm_res_kernel(a_ref, w_ref, *refs, coef, npt):
    *x_refs, gate_ref, o_ref = refs
    y = jnp.dot(a_ref[...], w_ref[...], preferred_element_type=F32)

    def body(x_ref):
        for r in range(y.shape[0] // CHUNK):
            rows = slice(r * CHUNK, (r + 1) * CHUNK)
            o_ref[rows, :] = x_ref[rows, :] + (coef * gate_ref[r:r + 1, :]) * y[rows, :]

    _on_source(x_refs, npt, body)


def _matmul_residual(a, w, lead, x, modg, layer, s, coef, sq, tile):
    M, K = a.shape
    N = w.shape[-1]
    tm = _pick(sq.Mg, tile, CHUNK * SUBLANE)
    tn = _pick(N, tile, LANE)
    ng = tm // CHUNK
    gcol = (s * 3 + 2) * (N // tn)
    x_specs, xs, npt = _x_specs(x, tm, tn, sq, lambda j: j)
    return pl.pallas_call(
        functools.partial(_mm_res_kernel, coef=coef, npt=npt),
        grid=(M // tm, N // tn),
        in_specs=[pl.BlockSpec((tm, K), lambda i, j: (i, 0)),
                  _w_spec(w, lead, K, tn, lambda j: j)] + x_specs +
                 [pl.BlockSpec((None, ng, tn), lambda i, j: (layer, i, gcol + j))],
        out_specs=pl.BlockSpec((tm, tn), lambda i, j: (i, j)),
        out_shape=jax.ShapeDtypeStruct((M, N), F32),
        compiler_params=_params(("parallel", "parallel")),
    )(a, w, *xs, modg)


def _gate_proj_kernel(a_ref, w_ref, p1_ref, p2_ref, o_ref, *, kind, H):
    y = jnp.dot(a_ref[...], w_ref[...], preferred_element_type=F32)
    col = lax.broadcasted_iota(jnp.int32, y.shape, 1)
    if kind == "gdn_gla":
        loga = -jnp.exp(p1_ref[...]) * _softplus(y + p2_ref[...])
        o_ref[...] = jnp.where(col < H, loga, jnp.where(col < 2 * H, _sigmoid(y), y))
    else:
        z = y + p1_ref[...]
        o_ref[...] = jnp.where(col < H, z, _log_sigmoid(z))


def _gate_proj(a, w, p1, p2, kind, H):
    M, K = a.shape
    N = w.shape[1]
    tm = _pick(M, 1024, CHUNK)
    return pl.pallas_call(
        functools.partial(_gate_proj_kernel, kind=kind, H=H),
        grid=(M // tm,),
        in_specs=[pl.BlockSpec((tm, K), lambda i: (i, 0)),
                  pl.BlockSpec((K, N), lambda i: (0, 0)),
                  pl.BlockSpec((1, N), lambda i: (0, 0)),
                  pl.BlockSpec((1, N), lambda i: (0, 0))],
        out_specs=pl.BlockSpec((tm, N), lambda i: (i, 0)),
        out_shape=jax.ShapeDtypeStruct((M, N), F32),
        compiler_params=_params(("parallel",)),
    )(a, w, p1, p2)


def _gdn_prep_kernel(x_ref, p_ref, cs_ref, cw_ref, gcol_ref, grow_ref,
                     wq_ref, u_ref, pk_out_ref, el_ref, nb_ref, in_scr, cv_scr, *, HB, dk, dv, sq):
    L = CHUNK
    g = pl.program_id(1)
    _, first, last, in_p = sq.pos(g)
    Wc = cw_ref.shape[0]
    npv = Wc - 1
    HK = HB * dk
    ncol = x_ref.shape[1]

    in_scr[8 - npv:8, :] = jnp.where(first, jnp.where(in_p, 0.0, cs_ref[0]), p_ref[8 - npv:8, :])
    in_scr[8:8 + L, :] = x_ref[...]
    cwid = _pick(ncol, 512, LANE)
    for c in range(0, ncol, cwid):
        cols = slice(c, c + cwid)
        y = in_scr[8 - npv:8 - npv + L, cols] * cw_ref[0:1, cols]
        for t in range(1, Wc):
            y = y + in_scr[8 - npv + t:8 - npv + t + L, cols] * cw_ref[t:t + 1, cols]
        cv_scr[:, cols] = y * _sigmoid(y)

    ii, jj = _tri_masks(L)
    incl = jj <= ii
    strict = jj < ii
    eye = (ii == jj).astype(F32)
    eye_b = eye.astype(BF16)
    gcol = gcol_ref[0]
    gc_all = _dot_hi(incl.astype(F32), gcol[:, :HB])
    gr_all = _dot_hi(grow_ref[0, 0], (ii <= jj).astype(F32))
    n_sq = int(np.log2(L)) - 1

    heads = range(HB)
    sks = [slice(j * dk, (j + 1) * dk) for j in heads]
    svs = [slice(j * dv, (j + 1) * dv) for j in heads]
    kn, gc, beta, eg, gL, kb, Ps, T = ([None] * HB for _ in range(8))
    for j in heads:
        q, k = cv_scr[:, sks[j]], cv_scr[:, HK + j * dk:HK + (j + 1) * dk]
        qn = (q * lax.rsqrt(jnp.sum(q * q, axis=-1, keepdims=True) + EPS)) * (dk ** -0.5)
        kn[j] = k * lax.rsqrt(jnp.sum(k * k, axis=-1, keepdims=True) + EPS)
        gc[j] = gc_all[:, j:j + 1]
        beta[j] = gcol[:, HB + j:HB + j + 1]
        dec_i = jnp.exp(jnp.where(incl, gc[j] - gr_all[j:j + 1, :], -jnp.inf))
        eg[j] = jnp.exp(gc[j])
        gL[j] = gc[j][L - 1:L, :]
        kb[j] = kn[j].astype(BF16)
        qb = qn.astype(BF16)
        wq_ref[0, L:2 * L, sks[j]] = (eg[j] * qn).astype(BF16)
        pk_out_ref[0, j, 0:L, :] = (dec_i * _dot_nt(qb, kb[j])).astype(BF16)
        A = beta[j] * jnp.where(strict, dec_i, 0.0) * _dot_nt(kb[j], kb[j])
        T[j] = eye - A
        Ps[j] = _split_bf16(-A)
    for _ in range(n_sq):
        for j in heads:
            Ps[j] = _split_bf16(_dot_x3(Ps[j], Ps[j]))
        for j in heads:
            T[j] = T[j] + _dot_x3(_split_bf16(T[j]), Ps[j])
    for j in heads:
        v = cv_scr[:, 2 * HK + j * dv:2 * HK + (j + 1) * dv]
        rhs = jnp.concatenate([(beta[j] * eg[j]) * kn[j], beta[j] * v], axis=1)
        WU = _dot(T[j], rhs)
        wq_ref[0, 0:L, sks[j]] = WU[:, :dk].astype(BF16)
        u_ref[:, svs[j]] = WU[:, dk:]
        kw = (kn[j] * jnp.exp(gL[j] - gc[j])).astype(BF16)
        pk_out_ref[0, j, L:L + dk, :] = _dot_tn(kw, eye_b).astype(BF16)
        el_ref[0, :, svs[j]] = jnp.broadcast_to(jnp.exp(gL[j]), (SUBLANE, dv))

    @pl.when(last)
    def _():
        nb_ref[0] = x_ref[L - npv:L, :]


def _gdn_prep(proj, gates_col, gates_row, conv_state, conv_w, H, dk, dv, sq):
    assert dk == dv
    HK = H * dk
    W3 = 3 * HK
    Bt = sq.Bp + sq.Bs
    Wc = conv_w.shape[0]
    npv = Wc - 1
    in_specs = [pl.BlockSpec((CHUNK, W3), lambda h, g: (g, 0)),
                pl.BlockSpec((SUBLANE, W3), lambda h, g: (jnp.maximum(g * (CHUNK // SUBLANE) - 1, 0), 0)),
                pl.BlockSpec((1, npv, W3), lambda h, g: (sq.seq_s(g), 0, 0)),
                pl.BlockSpec((Wc, W3), lambda h, g: (0, 0)),
                pl.BlockSpec((1, CHUNK, 2 * H), lambda h, g: (0, g, 0)),
                pl.BlockSpec((1, 1, H, CHUNK), lambda h, g: (g, 0, 0, 0))]
    out_specs = [pl.BlockSpec((1, 2 * CHUNK, HK), lambda h, g: (g, 0, 0)),
                 pl.BlockSpec((CHUNK, HK), lambda h, g: (g, 0)),
                 pl.BlockSpec((1, H, CHUNK + dk, CHUNK), lambda h, g: (g, 0, 0, 0)),
                 pl.BlockSpec((1, SUBLANE, HK), lambda h, g: (g, 0, 0)),
                 pl.BlockSpec((1, npv, W3), lambda h, g: (sq.pos(g)[0], 0, 0))]
    out_shape = [jax.ShapeDtypeStruct((sq.NG, 2 * CHUNK, HK), BF16),
                 jax.ShapeDtypeStruct((sq.M, HK), F32),
                 jax.ShapeDtypeStruct((sq.NG, H, CHUNK + dk, CHUNK), BF16),
                 jax.ShapeDtypeStruct((sq.NG, SUBLANE, HK), F32),
                 jax.ShapeDtypeStruct((Bt, npv, W3), F32)]
    return pl.pallas_call(
        functools.partial(_gdn_prep_kernel, HB=H, dk=dk, dv=dv, sq=sq),
        grid=(1, sq.NG),
        in_specs=in_specs,
        out_specs=out_specs,
        out_shape=out_shape,
        scratch_shapes=[pltpu.VMEM((SUBLANE + CHUNK, W3), F32), pltpu.VMEM((CHUNK, W3), F32)],
        compiler_params=_params(("parallel", "arbitrary")),
    )(proj, proj, conv_state, conv_w, gates_col, gates_row)


def _gdn_scan_kernel(wq_ref, u_ref, pk_ref, el_ref, ga_ref, s0_ref, ng_ref, buf_ref,
                     o_ref, sp_ref, ss_ref, s_scr, *, H, dk, dv, sq, axis=0):
    del buf_ref
    L = CHUNK
    g = pl.program_id(axis)
    _, first, last, in_p = sq.pos(g)

    @pl.when(first & in_p)
    def _():
        s_scr[...] = jnp.zeros_like(s_scr)

    @pl.when(first & jnp.logical_not(in_p))
    def _():
        s_scr[...] = s0_ref[0]

    heads = range(H)
    sks = [slice(j * dk, (j + 1) * dk) for j in heads]
    svs = [slice(j * dv, (j + 1) * dv) for j in heads]
    M1, M2 = [None] * H, [None] * H
    for j in heads:
        M1[j] = jnp.dot(wq_ref[0, :, sks[j]], s_scr[j].astype(BF16), preferred_element_type=F32)
    for j in heads:
        E = u_ref[:, svs[j]] - M1[j][:L]
        M2[j] = jnp.dot(pk_ref[0, j], E.astype(BF16), preferred_element_type=F32)
    for j in heads:
        s_scr[j] = el_ref[0, 0:1, svs[j]] * s_scr[j] + M2[j][L:]
        o = M1[j][L:] + M2[j][:L]
        on = (o * lax.rsqrt(jnp.mean(o * o, axis=-1, keepdims=True) + EPS)) * ng_ref[...]
        ga = ga_ref[:, svs[j]]
        o_ref[:, svs[j]] = (on * (ga * _sigmoid(ga))).astype(o_ref.dtype)

    @pl.when(last & in_p)
    def _():
        sp_ref[0] = s_scr[...]

    @pl.when(last & jnp.logical_not(in_p))
    def _():
        ss_ref[0] = s_scr[...]


def _gdn_scan(wq, u, pk, el, proj, gate_off, S0, norm_g, buf, H, dk, dv, sq):
    HK, HV = H * dk, H * dv
    in_specs = [pl.BlockSpec((1, 2 * CHUNK, HK), lambda g: (g, 0, 0)),
                pl.BlockSpec((CHUNK, HV), lambda g: (g, 0)),
                pl.BlockSpec((1, H, CHUNK + dk, CHUNK), lambda g: (g, 0, 0, 0)),
                pl.BlockSpec((1, SUBLANE, HV), lambda g: (g, 0, 0)),
                pl.BlockSpec((CHUNK, HV), lambda g: (g, gate_off // HV)),
                pl.BlockSpec((1, H, dk, dv), lambda g: (sq.seq_s(g), 0, 0, 0)),
                pl.BlockSpec((1, dv), lambda g: (0, 0)),
                pl.BlockSpec(memory_space=pl.ANY)]
    out_specs = [pl.BlockSpec((CHUNK, HV), lambda g: (g, 0)),
                 pl.BlockSpec((1, H, dk, dv), lambda g: (sq.seq_p(g), 0, 0, 0)),
                 pl.BlockSpec((1, H, dk, dv), lambda g: (sq.seq_s(g), 0, 0, 0))]
    out_shape = [jax.ShapeDtypeStruct(buf.shape, buf.dtype),
                 jax.ShapeDtypeStruct((sq.Bp, H, dk, dv), F32),
                 jax.ShapeDtypeStruct((sq.Bs, H, dk, dv), F32)]
    return pl.pallas_call(
        functools.partial(_gdn_scan_kernel, H=H, dk=dk, dv=dv, sq=sq),
        grid=(sq.NG,),
        in_specs=in_specs,
        out_specs=out_specs,
        out_shape=out_shape,
        scratch_shapes=[pltpu.VMEM((H, dk, dv), F32)],
        input_output_aliases={7: 0},
        compiler_params=_params(("arbitrary",)),
    )(wq, u, pk, el, proj, S0, norm_g.reshape(1, dv), buf)


def _gla_kernel(x_ref, lr_ref, w2_ref, gb_ref, s0_ref, ng_ref, buf_ref,
                o_ref, sp_ref, ss_ref, s_scr, *, HB, dk, dv, sq):
    del buf_ref
    HK, HV = HB * dk, HB * dv
    q_ref, k_ref = x_ref.at[:, 0:HK], x_ref.at[:, HK:2 * HK]
    v_ref, r_ref = x_ref.at[:, 2 * HK:2 * HK + HV], x_ref.at[:, 2 * HK + HV:2 * HK + 2 * HV]
    L = CHUNK
    g = pl.program_id(1)
    _, first, last, in_p = sq.pos(g)

    @pl.when(first & in_p)
    def _():
        s_scr[...] = jnp.zeros_like(s_scr)

    @pl.when(first & jnp.logical_not(in_p))
    def _():
        s_scr[...] = s0_ref[0]

    ii, jj = _tri_masks(L)
    incl = jj <= ii
    tri = incl.astype(F32)
    krow = lax.broadcasted_iota(jnp.int32, (L, dk), 0)
    di = lax.broadcasted_iota(jnp.int32, (dk, dk), 0)
    dj = lax.broadcasted_iota(jnp.int32, (dk, dk), 1)
    lr = lr_ref[...]

    heads = range(HB)
    sks = [slice(j * dk, (j + 1) * dk) for j in heads]
    svs = [slice(j * dv, (j + 1) * dv) for j in heads]
    G, att = [None] * HB, [None] * HB
    for j in heads:
        x = _dot(lr, w2_ref[:, sks[j]]) + gb_ref[:, sks[j]]
        G[j] = _dot_hi(tri, _log_sigmoid(x) / GLA_TAU)
    for j in heads:
        q = q_ref[:, sks[j]] * (dk ** -0.5)
        k = k_ref[:, sks[j]]
        rows = []
        for b in range(L // GLA_SUB):
            lo, hi = b * GLA_SUB, (b + 1) * GLA_SUB
            mid = lo + GLA_SUB // 2
            ref_g = G[j][mid:mid + 1, :]
            qg = q[lo:hi, :] * jnp.exp(G[j][lo:hi, :] - ref_g)
            kg = k * jnp.exp(jnp.where(krow < hi, ref_g - G[j], -jnp.inf))
            rows.append(_dot_nt(qg, kg))
        att[j] = jnp.where(incl, jnp.concatenate(rows, axis=0), 0.0)
    for j in heads:
        q = q_ref[:, sks[j]] * (dk ** -0.5)
        o = _dot(q * jnp.exp(G[j]), s_scr[j]) + _dot(att[j], v_ref[:, svs[j]])
        on = (o * lax.rsqrt(jnp.mean(o * o, axis=-1, keepdims=True) + EPS)) * ng_ref[...]
        r = r_ref[:, svs[j]]
        o_ref[:, svs[j]] = (on * (r * _sigmoid(r))).astype(o_ref.dtype)
    for j in heads:
        GL = G[j][L - 1:L, :]
        kS = k_ref[:, sks[j]] * jnp.exp(GL - G[j])
        a_col = jnp.sum(jnp.where(di == dj, jnp.broadcast_to(jnp.exp(GL), (dk, dk)), 0.0),
                        axis=1, keepdims=True)
        s_scr[j] = a_col * s_scr[j] + _dot_tn(kS, v_ref[:, svs[j]])

    @pl.when(last & in_p)
    def _():
        sp_ref[0] = s_scr[...]

    @pl.when(last & jnp.logical_not(in_p))
    def _():
        ss_ref[0] = s_scr[...]


def _gla(proj, lr, w2, gb, S0, norm_g, buf, buf_col_off, H, dk, dv, sq):
    HG, HB = 1, H
    WK, WV = HB * dk, HB * dv
    R = lr.shape[1]
    assert proj.shape[1] == 2 * WK + 2 * WV and buf_col_off % WV == 0

    in_specs = [pl.BlockSpec((CHUNK, 2 * WK + 2 * WV), lambda h, g: (g, 0)),
                pl.BlockSpec((CHUNK, R), lambda h, g: (g, 0)),
                pl.BlockSpec((R, WK), lambda h, g: (0, h)),
                pl.BlockSpec((1, WK), lambda h, g: (0, h)),
                pl.BlockSpec((1, HB, dk, dv), lambda h, g: (sq.seq_s(g), h, 0, 0)),
                pl.BlockSpec((1, dv), lambda h, g: (0, 0)),
                pl.BlockSpec(memory_space=pl.ANY)]
    out_specs = [pl.BlockSpec((CHUNK, WV), lambda h, g: (g, buf_col_off // WV + h)),
                 pl.BlockSpec((1, HB, dk, dv), lambda h, g: (sq.seq_p(g), h, 0, 0)),
                 pl.BlockSpec((1, HB, dk, dv), lambda h, g: (sq.seq_s(g), h, 0, 0))]
    out_shape = [jax.ShapeDtypeStruct(buf.shape, buf.dtype),
                 jax.ShapeDtypeStruct((sq.Bp, H, dk, dv), F32),
                 jax.ShapeDtypeStruct((sq.Bs, H, dk, dv), F32)]
    return pl.pallas_call(
        functools.partial(_gla_kernel, HB=HB, dk=dk, dv=dv, sq=sq),
        grid=(HG, sq.NG),
        in_specs=in_specs,
        out_specs=out_specs,
        out_shape=out_shape,
        scratch_shapes=[pltpu.VMEM((HB, dk, dv), F32)],
        input_output_aliases={6: 0},
        compiler_params=_params(("parallel", "arbitrary")),
    )(proj, lr, w2, gb.reshape(1, -1), S0, norm_g.reshape(1, dv), buf)


def _scan_gla_kernel(wq_ref, u_ref, pk_ref, el_ref, ga_ref, s0_ref, ng_ref,
                     x_ref, lr_ref, w2_ref, gb_ref, ls0_ref, lng_ref,
                     o_ref, sp_ref, ss_ref, lsp_ref, lss_ref, s_scr, ls_scr,
                     *, H, dk, dv, Hl, dkl, dvl, sq):
    Wg = H * dv
    _gdn_scan_kernel(wq_ref, u_ref, pk_ref, el_ref, ga_ref, s0_ref, ng_ref, None,
                     o_ref.at[:, 0:Wg], sp_ref, ss_ref, s_scr, H=H, dk=dk, dv=dv, sq=sq, axis=1)
    _gla_kernel(x_ref, lr_ref, w2_ref, gb_ref, ls0_ref, lng_ref, None,
                o_ref.at[:, Wg:Wg + Hl * dvl], lsp_ref, lss_ref, ls_scr, HB=Hl, dk=dkl, dv=dvl, sq=sq)


def _gdn_scan_gla(wq, u, pk, el, projA, gate_off, S0, norm_g, projB, lr, w2, gb, lS0, lnorm_g,
                  H, dk, dv, Hl, dkl, dvl, sq):
    HK, HV = H * dk, H * dv
    WK, WV = Hl * dkl, Hl * dvl
    R = lr.shape[1]
    assert projB.shape[1] == 2 * WK + 2 * WV and gate_off % HV == 0
    in_specs = [pl.BlockSpec((1, 2 * CHUNK, HK), lambda h, g: (g, 0, 0)),
                pl.BlockSpec((CHUNK, HV), lambda h, g: (g, 0)),
                pl.BlockSpec((1, H, CHUNK + dk, CHUNK), lambda h, g: (g, 0, 0, 0)),
                pl.BlockSpec((1, SUBLANE, HV), lambda h, g: (g, 0, 0)),
                pl.BlockSpec((CHUNK, HV), lambda h, g: (g, gate_off // HV)),
                pl.BlockSpec((1, H, dk, dv), lambda h, g: (sq.seq_s(g), 0, 0, 0)),
                pl.BlockSpec((1, dv), lambda h, g: (0, 0)),
                pl.BlockSpec((CHUNK, 2 * WK + 2 * WV), lambda h, g: (g, 0)),
                pl.BlockSpec((CHUNK, R), lambda h, g: (g, 0)),
                pl.BlockSpec((R, WK), lambda h, g: (0, 0)),
                pl.BlockSpec((1, WK), lambda h, g: (0, 0)),
                pl.BlockSpec((1, Hl, dkl, dvl), lambda h, g: (sq.seq_s(g), 0, 0, 0)),
                pl.BlockSpec((1, dvl), lambda h, g: (0, 0))]
    out_specs = [pl.BlockSpec((CHUNK, HV + WV), lambda h, g: (g, 0)),
                 pl.BlockSpec((1, H, dk, dv), lambda h, g: (sq.seq_p(g), 0, 0, 0)),
                 pl.BlockSpec((1, H, dk, dv), lambda h, g: (sq.seq_s(g), 0, 0, 0)),
                 pl.BlockSpec((1, Hl, dkl, dvl), lambda h, g: (sq.seq_p(g), 0, 0, 0)),
                 pl.BlockSpec((1, Hl, dkl, dvl), lambda h, g: (sq.seq_s(g), 0, 0, 0))]
    out_shape = [jax.ShapeDtypeStruct((sq.M, HV + WV), BF16),
                 jax.ShapeDtypeStruct((sq.Bp, H, dk, dv), F32),
                 jax.ShapeDtypeStruct((sq.Bs, H, dk, dv), F32),
                 jax.ShapeDtypeStruct((sq.Bp, Hl, dkl, dvl), F32),
                 jax.ShapeDtypeStruct((sq.Bs, Hl, dkl, dvl), F32)]
    return pl.pallas_call(
        functools.partial(_scan_gla_kernel, H=H, dk=dk, dv=dv, Hl=Hl, dkl=dkl, dvl=dvl, sq=sq),
        grid=(1, sq.NG),
        in_specs=in_specs,
        out_specs=out_specs,
        out_shape=out_shape,
        scratch_shapes=[pltpu.VMEM((H, dk, dv), F32), pltpu.VMEM((Hl, dkl, dvl), F32)],
        compiler_params=_params(("parallel", "arbitrary")),
    )(wq, u, pk, el, projA, S0, norm_g.reshape(1, dv), projB, lr, w2, gb.reshape(1, -1), lS0,
      lnorm_g.reshape(1, dvl))


def _mlstm_kernel(x_ref, gcol_ref, grow_ref, c0_ref, n0_ref, m0_ref, ng_ref,
                  o_ref, cp_ref, cs_ref, np_ref, ns_ref, mp_ref, ms_ref, c_scr, n_scr, m_scr,
                  *, HB, dk, dv, sq):
    HK, HV = HB * dk, HB * dv
    q_ref, k_ref = x_ref.at[:, 0:HK], x_ref.at[:, HK:2 * HK]
    v_ref, og_ref = x_ref.at[:, 2 * HK:2 * HK + HV], x_ref.at[:, 2 * HK + HV:2 * HK + 2 * HV]
    L = CHUNK
    g = pl.program_id(1)
    _, first, last, in_p = sq.pos(g)

    @pl.when(first & in_p)
    def _():
        c_scr[...] = jnp.zeros_like(c_scr)
        n_scr[...] = jnp.zeros_like(n_scr)
        m_scr[...] = jnp.zeros_like(m_scr)

    @pl.when(first & jnp.logical_not(in_p))
    def _():
        c_scr[...] = c0_ref[0]
        n_scr[...] = n0_ref[0, 0]
        m_scr[...] = m0_ref[0, 0]

    ii, jj = _tri_masks(L)
    incl = jj <= ii
    gcol = gcol_ref[0]
    grow = grow_ref[0, 0]
    bc_all = _dot_hi(incl.astype(F32), gcol[:, HB:])
    br_all = _dot_hi(grow[HB:, :], (ii <= jj).astype(F32))

    heads = range(HB)
    sks = [slice(j * dk, (j + 1) * dk) for j in heads]
    svs = [slice(j * dv, (j + 1) * dv) for j in heads]
    mi, w_inter, expD, wk, decay, qb, Wm = ([None] * HB for _ in range(7))
    for j in heads:
        bc = bc_all[:, j:j + 1]
        m_prev = m_scr[j:j + 1, :]
        Dlog = jnp.where(incl, bc - br_all[j:j + 1, :] + grow[j:j + 1, :], -jnp.inf)
        inter = bc + m_prev
        mi[j] = jnp.maximum(inter, jnp.max(Dlog, axis=-1, keepdims=True))
        w_inter[j] = jnp.exp(inter - mi[j])
        expD[j] = jnp.exp(Dlog - mi[j])
        mL = mi[j][L - 1:L, :]
        bL = bc[L - 1:L, :]
        wk[j] = jnp.exp(bL - bc + gcol[:, j:j + 1] - mL)
        decay[j] = jnp.exp(bL + m_prev - mL)
        m_scr[j:j + 1, :] = mL
    for j in heads:
        qb[j] = (q_ref[:, sks[j]] * (dk ** -0.5)).astype(BF16)
        Wm[j] = expD[j] * _dot_nt(qb[j], k_ref[:, sks[j]])
    for j in heads:
        q = q_ref[:, sks[j]] * (dk ** -0.5)
        num = w_inter[j] * _dot(qb[j], c_scr[j]) + _dot(Wm[j], v_ref[:, svs[j]])
        den = (w_inter[j] * jnp.sum(q * n_scr[j:j + 1, :], axis=-1, keepdims=True)
               + jnp.sum(Wm[j], axis=-1, keepdims=True))
        h = num / jnp.maximum(jnp.abs(den), jnp.exp(-mi[j]))
        hn = (h * lax.rsqrt(jnp.mean(h * h, axis=-1, keepdims=True) + EPS)) * ng_ref[...]
        o_ref[:, svs[j]] = (hn * _sigmoid(og_ref[:, svs[j]])).astype(o_ref.dtype)
    for j in heads:
        kw = k_ref[:, sks[j]] * wk[j]
        c_scr[j] = decay[j] * c_scr[j] + _dot_tn(kw, v_ref[:, svs[j]])
        n_scr[j:j + 1, :] = decay[j] * n_scr[j:j + 1, :] + jnp.sum(kw, axis=0, keepdims=True)

    @pl.when(last & in_p)
    def _():
        cp_ref[0] = c_scr[...]
        np_ref[0, 0] = n_scr[...]
        mp_ref[0, 0] = m_scr[...]

    @pl.when(last & jnp.logical_not(in_p))
    def _():
        cs_ref[0] = c_scr[...]
        ns_ref[0, 0] = n_scr[...]
        ms_ref[0, 0] = m_scr[...]


def _mlstm(proj, gates_col, gates_row, C0, n0, m0, norm_g, H, dk, dv, sq):
    HG, HB = 1, H
    WK, WV = HB * dk, HB * dv
    assert proj.shape[1] == 2 * WK + 2 * WV

    def st(shape, seq_of):
        return pl.BlockSpec((1,) + shape, lambda h, g: (seq_of(g), h) + (0,) * (len(shape) - 1))

    in_specs = [pl.BlockSpec((CHUNK, 2 * WK + 2 * WV), lambda h, g: (g, 0)),
                pl.BlockSpec((1, CHUNK, 2 * HB), lambda h, g: (h, g, 0)),
                pl.BlockSpec((1, 1, 2 * HB, CHUNK), lambda h, g: (g, h, 0, 0)),
                st((HB, dk, dv), sq.seq_s), st((1, HB, dk), sq.seq_s), st((1, HB, 1), sq.seq_s),
                pl.BlockSpec((1, dv), lambda h, g: (0, 0))]
    out_specs = [pl.BlockSpec((CHUNK, WV), lambda h, g: (g, h)),
                 st((HB, dk, dv), sq.seq_p), st((HB, dk, dv), sq.seq_s),
                 st((1, HB, dk), sq.seq_p), st((1, HB, dk), sq.seq_s),
                 st((1, HB, 1), sq.seq_p), st((1, HB, 1), sq.seq_s)]
    out_shape = [jax.ShapeDtypeStruct((sq.M, H * dv), BF16),
                 jax.ShapeDtypeStruct((sq.Bp, H, dk, dv), F32),
                 jax.ShapeDtypeStruct((sq.Bs, H, dk, dv), F32),
                 jax.ShapeDtypeStruct((sq.Bp, HG, HB, dk), F32),
                 jax.ShapeDtypeStruct((sq.Bs, HG, HB, dk), F32),
                 jax.ShapeDtypeStruct((sq.Bp, HG, HB, 1), F32),
                 jax.ShapeDtypeStruct((sq.Bs, HG, HB, 1), F32)]
    return pl.pallas_call(
        functools.partial(_mlstm_kernel, HB=HB, dk=dk, dv=dv, sq=sq),
        grid=(HG, sq.NG),
        in_specs=in_specs,
        out_specs=out_specs,
        out_shape=out_shape,
        scratch_shapes=[pltpu.VMEM((HB, dk, dv), F32),
                        pltpu.VMEM((HB, dk), F32),
                        pltpu.VMEM((HB, 1), F32)],
        compiler_params=_params(("parallel", "arbitrary")),
    )(proj, gates_col, gates_row, C0,
      n0.reshape(sq.Bs, HG, HB, dk), m0.reshape(sq.Bs, HG, HB, 1), norm_g.reshape(1, dv))


def _gates_col(g, n_kinds, HG, HB):
    M = g.shape[0]
    return g.reshape(M, n_kinds, HG, HB).transpose(2, 0, 1, 3).reshape(HG, M, n_kinds * HB)


def _gates_row(g, n_kinds, HG, HB):
    M = g.shape[0]
    g = g.reshape(M // CHUNK, CHUNK, n_kinds, HG, HB).transpose(0, 3, 2, 4, 1)
    return g.reshape(M // CHUNK, HG, n_kinds * HB, CHUNK)


def kernel(x_prompt, x_sample, state_gdn_S, state_gdn_conv, state_gla_S, state_mlstm_C, state_mlstm_n,
           state_mlstm_m, c_prompt, c_sample, ada_w, ada_b, norm_g, ffn_w_in, ffn_w_out, gdn_gla_w_in,
           gdn_gla_w_out, gdn_conv_w, gdn_A_log, gdn_dt_bias, gdn_norm_g, gla_gate_w2, gla_gate_b,
           gla_norm_g, mlstm_w_in, mlstm_w_out, mlstm_gate_b, mlstm_norm_g, final_norm_g):
    Bp, Tp, D = x_prompt.shape
    Bs, Ts, _ = x_sample.shape
    depth = ada_w.shape[0]
    assert Tp % CHUNK == 0 and Ts % CHUNK == 0 and norm_g.shape[1] == 3
    sq = _Seqs(Bp, Tp // CHUNK, Bs, Ts // CHUNK)
    Mp, Ms = sq.Mp, sq.Ms

    _, _, Hg, dkg, dvg = state_gdn_S.shape
    _, _, Hl, dkl, dvl = state_gla_S.shape
    _, _, Hm, dkm, dvm = state_mlstm_C.shape
    Wg, Wl, Wm = Hg * dvg, Hl * dvl, Hm * dvm
    R = gla_gate_w2.shape[1]
    npv = state_gdn_conv.shape[2]
    assert npv + 1 == gdn_conv_w.shape[1] and npv <= SUBLANE and npv <= CHUNK

    ffn_w_out_b = ffn_w_out.astype(BF16)
    gdn_gla_w_in_b = gdn_gla_w_in.astype(BF16)
    gdn_gla_w_out_b = gdn_gla_w_out.astype(BF16)
    mlstm_w_in_b = mlstm_w_in.astype(BF16)
    mlstm_w_out_b = mlstm_w_out.astype(BF16)

    modg = _ada_mod(jnp.concatenate([c_prompt, c_sample], axis=0), ada_w, ada_b, sq)

    def ffn(x, layer, f, s):
        h = _prenorm(x, norm_g[layer, s], modg, layer, s, sq)
        hh = _matmul_swiglu(h, ffn_w_in, (layer, f))
        return _matmul_residual(hh, ffn_w_out_b, (layer, f), x, modg, layer, s, 0.5, sq, 512)

    x = (x_prompt.reshape(Mp, D), x_sample.reshape(Ms, D))
    new_gS, new_conv, new_lS, new_C, new_n, new_m = [], [], [], [], [], []
    for layer in range(depth):
        x = ffn(x, layer, 0, 0)
        i = layer // 2
        h = _prenorm(x, norm_g[layer, 1], modg, layer, 1, sq)
        if layer % 2 == 0:
            nqk = Hg * dkg
            nA = 2 * nqk + 2 * Wg
            nB = 2 * Hl * dkl + 2 * Wl
            oB = nA + 2 * Hg
            wb = gdn_gla_w_in_b[i]
            w_gate = jnp.concatenate([wb[:, nA:oB], wb[:, oB + nB:]], axis=1)
            ngc = w_gate.shape[1]
            p1 = jnp.zeros((1, ngc), F32).at[0, :Hg].set(gdn_A_log[i].astype(F32))
            p2 = jnp.zeros((1, ngc), F32).at[0, :Hg].set(gdn_dt_bias[i].astype(F32))
            projA = _matmul(h, gdn_gla_w_in_b, (i,), nA, F32)
            projB = _matmul(h, wb[:, oB:oB + nB], (), nB, F32)
            gates = _gate_proj(h, w_gate, p1, p2, "gdn_gla", Hg)
            gcol = _gates_col(gates[:, :2 * Hg], 2, 1, Hg)
            grow = _gates_row(gates[:, :Hg], 1, 1, Hg)
            wq, u, pk, el, nb = _gdn_prep(
                projA, gcol, grow, state_gdn_conv[i].astype(F32), gdn_conv_w[i].astype(F32),
                Hg, dkg, dvg, sq)
            buf, gS_p, gS_s, lS_p, lS_s = _gdn_scan_gla(
                wq, u, pk, el, projA, 2 * nqk + Wg, state_gdn_S[i].astype(F32), gdn_norm_g[i].astype(F32),
                projB, gates[:, 2 * Hg:], gla_gate_w2[i].astype(F32), gla_gate_b[i].astype(F32),
                state_gla_S[i].astype(F32), gla_norm_g[i].astype(F32), Hg, dkg, dvg, Hl, dkl, dvl, sq)
            new_gS.append((gS_p, gS_s))
            new_conv.append((nb[:Bp], nb[Bp:]))
            new_lS.append((lS_p, lS_s))
            w_out, lead = gdn_gla_w_out_b, (i,)
        else:
            nqk = Hm * dkm
            nmain = 2 * nqk + 2 * Wm
            gb = mlstm_gate_b[i].astype(F32).reshape(1, 2 * Hm)
            proj = _matmul(h, mlstm_w_in_b, (i,), nmain, F32)
            gates = _gate_proj(h, mlstm_w_in_b[i][:, nmain:], gb, gb, "mlstm", Hm)
            gcol = _gates_col(gates, 2, 1, Hm)
            grow = _gates_row(gates, 2, 1, Hm)
            buf, C_p, C_s, n_p, n_s, m_p, m_s = _mlstm(
                proj, gcol, grow, state_mlstm_C[i].astype(F32), state_mlstm_n[i].astype(F32),
                state_mlstm_m[i].astype(F32), mlstm_norm_g[i].astype(F32), Hm, dkm, dvm, sq)
            new_C.append((C_p, C_s))
            new_n.append((n_p.reshape(Bp, Hm, dkm), n_s.reshape(Bs, Hm, dkm)))
            new_m.append((m_p.reshape(Bp, Hm), m_s.reshape(Bs, Hm)))
            w_out, lead = mlstm_w_out_b, (i,)
        x = _matmul_residual(buf, w_out, lead, x, modg, layer, 1, 1.0, sq, 1024)
        x = ffn(x, layer, 1, 2)

    y_p = _final_norm(x, final_norm_g, 0, Mp).reshape(Bp, Tp, D)
    y_s = _final_norm(x, final_norm_g, Mp, Ms).reshape(Bs, Ts, D)

    def pair(states):
        return jnp.stack([p for p, _ in states]), jnp.stack([s for _, s in states])

    gS_p, gS_s = pair(new_gS)
    gc_p, gc_s = pair(new_conv)
    lS_p, lS_s = pair(new_lS)
    C_p, C_s = pair(new_C)
    n_p, n_s = pair(new_n)
    m_p, m_s = pair(new_m)
    return (y_p, y_s, gS_p, gS_s, gc_p, gc_s, lS_p, lS_s, C_p, C_s, n_p, n_s, m_p, m_s)
```
